```python
import math
import jax
import jax.numpy as jnp
from jax import lax
import numpy as np

D_MODEL = 1024
BATCH = 8
SEQ = 2048
DEPTH = 4

CHUNK = 64
N_META = 16
EPS = 1e-6
N_BRANCH = 3

A_HEAD = 64
A_WIDTH = D_MODEL
A_HEADS = A_WIDTH // A_HEAD
A_DECAY_LORA = 64
A_ICLR_LORA = 64
A_VRES_LORA = 32
A_GATE_LORA = 128
A_LN_EPS = 64e-5
A_IN = 3 * A_WIDTH + A_DECAY_LORA + A_ICLR_LORA + A_GATE_LORA

B_WIDTH = 2 * D_MODEL
B_HEAD = 64
B_HEADS = B_WIDTH // B_HEAD
B_GROUPS = 4
B_HEADS_PER_GROUP = B_HEADS // B_GROUPS
B_STATE = 128
B_CONV = 4
B_CONV_DIM = B_WIDTH + 2 * B_GROUPS * B_STATE
B_IN = B_WIDTH + B_CONV_DIM + B_HEADS

C_HEADS = 8
C_QK_HEAD = 64
C_V_HEAD = 128
C_QK = C_HEADS * C_QK_HEAD
C_WIDTH = C_HEADS * C_V_HEAD
C_IN = 2 * C_QK + 2 * C_WIDTH
ROPE_BASE = 10000.0

GATE_IN = N_BRANCH * D_MODEL
W_IN = A_IN + B_IN + C_IN + GATE_IN
MIX_WIDTH = A_WIDTH + B_WIDTH + C_WIDTH

FFN_HIDDEN = 2816
FFN_CONV = 3

kernel_name = "hybrid_rwkv7_ssd_retention_convffn"


def split_last(t, sizes):
    offs = np.cumsum([0] + list(sizes)).tolist()
    return [t[..., offs[i]:offs[i + 1]] for i in range(len(sizes))]


def rms_norm(x, g):
    xf = x.astype(jnp.float32)
    y = xf * lax.rsqrt(jnp.mean(xf * xf, axis=-1, keepdims=True) + EPS)
    return (y * g.astype(jnp.float32)).astype(x.dtype)


def head_norm(y, eps):
    yc = y - jnp.mean(y, axis=-1, keepdims=True)
    return yc * lax.rsqrt(jnp.mean(yc * yc, axis=-1, keepdims=True) + eps)


def token_shift(p, mu):
    prev = jnp.pad(p, ((0, 0), (1, 0), (0, 0)))[:, :-1]
    return p + (prev - p) * mu


def causal_dwconv(u, w, b):
    width, ch = w.shape
    out = lax.conv_general_dilated(u, w[:, None, :].astype(u.dtype), window_strides=(1,),
                                   padding=[(width - 1, 0)],
                                   dimension_numbers=("NWC", "WIO", "NWC"),
                                   feature_group_count=ch)
    return out + b.astype(u.dtype)


def front_pad_chunks(t):
    pad = (-t.shape[1]) % CHUNK
    return jnp.pad(t, [(0, 0), (pad, 0)] + [(0, 0)] * (t.ndim - 2)), pad


def rotary(t, cos, sin):
    c, s = cos[None, :, None, :], sin[None, :, None, :]
    t1, t2 = t[..., 0::2], t[..., 1::2]
    return jnp.stack([t1 * c - t2 * s, t2 * c + t1 * s], axis=-1).reshape(t.shape)


def rwkv7_mix(p, mu, w0, w2, a0, a2, g2, k_k, k_a, r_k, ln_g, ln_b, v_first, vres):
    bsz, L, _ = p.shape
    f32 = jnp.float32
    ps = token_shift(p, mu)
    r, k, v, w_lo, a_lo, g_lo = split_last(ps, [A_WIDTH] * 3 + [A_DECAY_LORA, A_ICLR_LORA, A_GATE_LORA])
    w_log = -jax.nn.softplus(-(w0 + jnp.tanh(w_lo) @ w2)) - 0.5
    decay = jnp.exp(-jnp.exp(w_log.astype(f32)))
    a = jax.nn.sigmoid(a0 + a_lo @ a2)
    g = jax.nn.sigmoid(g_lo) @ g2
    if vres is None:
        v_first = v
    else:
        p_v, mu_v, v0, v2 = vres
        v = v + (v_first - v) * jax.nn.sigmoid(v0 + token_shift(p_v, mu_v) @ v2)

    def heads(t):
        return t.reshape(bsz, L, A_HEADS, A_HEAD).astype(f32)

    kk = heads(k * k_k)
    kk = kk * lax.rsqrt(jnp.maximum(jnp.sum(kk * kk, axis=-1, keepdims=True), 1e-24))
    a_h = heads(a)
    k_h = heads(k * (1.0 + (a - 1.0) * k_a))
    r_h, v_h, w_h = heads(r), heads(v), heads(decay)

    def step(state, inp):
        r_t, w_t, k_t, v_t, kk_t, a_t = inp
        sa = jnp.einsum("bhvk,bhk->bhv", state, -kk_t)
        state = (state * w_t[:, :, None, :] + sa[..., None] * (kk_t * a_t)[:, :, None, :]
                 + v_t[..., None] * k_t[:, :, None, :])
        return state, jnp.einsum("bhvk,bhk->bhv", state, r_t)

    xs = tuple(jnp.moveaxis(t, 1, 0) for t in (r_h, w_h, k_h, v_h, kk, a_h))
    s0 = jnp.zeros((bsz, A_HEADS, A_HEAD, A_HEAD), f32)
    _, y = lax.scan(step, s0, xs)
    y = jnp.moveaxis(y, 0, 1)
    y = head_norm(y, A_LN_EPS).reshape(bsz, L, A_WIDTH) * ln_g + ln_b
    bonus = jnp.sum(r_h * k_h * r_k, axis=-1, keepdims=True) * v_h
    y = (y + bonus.reshape(bsz, L, A_WIDTH)) * g
    return y.astype(p.dtype), v_first


def ssd_chunked(xdt, a, bm, cm):
    bsz = xdt.shape[0]
    xdt, pad = front_pad_chunks(xdt)
    a, _ = front_pad_chunks(a)
    bm, _ = front_pad_chunks(bm)
    cm, _ = front_pad_chunks(cm)
    nc = xdt.shape[1] // CHUNK
    x = xdt.reshape(bsz, nc, CHUNK, B_GROUPS, B_HEADS_PER_GROUP, B_HEAD)
    a = a.reshape(bsz, nc, CHUNK, B_GROUPS, B_HEADS_PER_GROUP)
    bm = bm.reshape(bsz, nc, CHUNK, B_GROUPS, B_STATE)
    cm = cm.reshape(bsz, nc, CHUNK, B_GROUPS, B_STATE)
    a_cs = jnp.cumsum(a, axis=2)
    causal = jnp.tril(jnp.ones((CHUNK, CHUNK), dtype=bool))
    seg = a_cs[:, :, :, None] - a_cs[:, :, None, :]
    lmat = jnp.exp(jnp.where(causal[None, None, :, :, None, None], seg, -jnp.inf))
    cb = jnp.einsum("bclgn,bcsgn->bclsg", cm, bm)
    y_diag = jnp.einsum("bclsg,bclsgr,bcsgrp->bclgrp", cb, lmat, x)
    decay_to_end = jnp.exp(a_cs[:, :, -1:] - a_cs)
    chunk_states = jnp.einsum("bclgn,bclgr,bclgrp->cbgrpn", bm, decay_to_end, x)
    chunk_decay = jnp.moveaxis(jnp.exp(a_cs[:, :, -1]), 1, 0)

    def step(h, inp):
        s_c, d_c = inp
        return h * d_c[..., None, None] + s_c, h

    h0 = jnp.zeros((bsz, B_GROUPS, B_HEADS_PER_GROUP, B_HEAD, B_STATE), x.dtype)
    _, h_in = lax.scan(step, h0, (chunk_states, chunk_decay))
    y_off = jnp.einsum("bclgn,cbgrpn,bclgr->bclgrp", cm, h_in, jnp.exp(a_cs))
    y = (y_diag + y_off).reshape(bsz, nc * CHUNK, B_HEADS, B_HEAD)
    return y[:, pad:]


def mamba2_mix(p, conv_w, conv_b, dt_bias, a_log, d_skip, norm_g):
    bsz, L, _ = p.shape
    f32 = jnp.float32
    z, xbc, dt_raw = split_last(p, [B_WIDTH, B_CONV_DIM, B_HEADS])
    xbc = jax.nn.silu(causal_dwconv(xbc, conv_w, conv_b))
    xs, bm, cm = split_last(xbc, [B_WIDTH, B_GROUPS * B_STATE, B_GROUPS * B_STATE])
    dt = jax.nn.softplus((dt_raw + dt_bias).astype(f32))
    a_neg = -jnp.exp(a_log.astype(f32))
    xh = xs.reshape(bsz, L, B_HEADS, B_HEAD).astype(f32)
    y = ssd_chunked(xh * dt[..., None], dt * a_neg,
                    bm.reshape(bsz, L, B_GROUPS, B_STATE).astype(f32),
                    cm.reshape(bsz, L, B_GROUPS, B_STATE).astype(f32))
    y = y + xh * d_skip.astype(f32)[:, None]
    y = y.reshape(bsz, L, B_WIDTH) * jax.nn.silu(z.astype(f32))
    yg = y.reshape(bsz, L, B_GROUPS, B_WIDTH // B_GROUPS)
    yg = yg * lax.rsqrt(jnp.mean(yg * yg, axis=-1, keepdims=True) + EPS)
    return (yg.reshape(bsz, L, B_WIDTH) * norm_g.astype(f32)).astype(p.dtype)


def retention_mix(p, cos, sin):
    bsz, L, _ = p.shape
    f32 = jnp.float32
    q, k, v, g = split_last(p, [C_QK, C_QK, C_WIDTH, C_WIDTH])
    q = rotary(q.reshape(bsz, L, C_HEADS, C_QK_HEAD).astype(f32), cos, sin)
    k = rotary(k.reshape(bsz, L, C_HEADS, C_QK_HEAD).astype(f32), cos, sin) * (C_QK_HEAD ** -0.5)
    v = v.reshape(bsz, L, C_HEADS, C_V_HEAD).astype(f32)
    q, pad = front_pad_chunks(q)
    k, _ = front_pad_chunks(k)
    v, _ = front_pad_chunks(v)
    nc = q.shape[1] // CHUNK
    q = q.reshape(bsz, nc, CHUNK, C_HEADS, C_QK_HEAD)
    k = k.reshape(bsz, nc, CHUNK, C_HEADS, C_QK_HEAD)
    v = v.reshape(bsz, nc, CHUNK, C_HEADS, C_V_HEAD)
    log_gamma = jnp.log(1.0 - 2.0 ** (-5.0 - jnp.arange(C_HEADS, dtype=f32)))
    idx = jnp.arange(CHUNK, dtype=f32)
    rel = idx[:, None] - idx[None, :]
    d_intra = jnp.where(rel >= 0, jnp.exp(jnp.maximum(rel, 0.0) * log_gamma[:, None, None]), 0.0)
    scores = jnp.einsum("bclhd,bcshd->bchls", q, k) * d_intra
    y_intra = jnp.einsum("bchls,bcshe->bclhe", scores, v)
    k_decay = jnp.exp((CHUNK - 1.0 - idx)[None, :] * log_gamma[:, None])
    kv = jnp.einsum("bclhd,bclhe,hl->cbhde", k, v, k_decay)
    chunk_decay = jnp.exp(CHUNK * log_gamma)[:, None, None]

    def step(s, kv_c):
        return s * chunk_decay + kv_c, s

    s0 = jnp.zeros((bsz, C_HEADS, C_QK_HEAD, C_V_HEAD), f32)
    _, s_in = lax.scan(step, s0, kv)
    q_decay = jnp.exp((idx + 1.0)[None, :] * log_gamma[:, None])
    y_cross = jnp.einsum("bclhd,cbhde,hl->bclhe", q, s_in, q_decay)
    y = (y_intra + y_cross).reshape(bsz, nc * CHUNK, C_HEADS, C_V_HEAD)[:, pad:]
    y = head_norm(y, EPS).reshape(bsz, L, C_WIDTH)
    return (jax.nn.silu(g.astype(f32)) * y).astype(p.dtype)


def setup_inputs(seed: int = 0) -> dict:
    key = jax.random.key(seed)
    ks = iter(jax.random.split(key, 48))
    f32 = jnp.float32
    D = D_MODEL
    L1 = DEPTH - 1

    def nrm(shape, scale):
        return jax.random.normal(next(ks), shape, f32) * scale

    def uni(shape, lo, hi):
        return jax.random.uniform(next(ks), shape, f32, lo, hi)

    x = nrm((BATCH, SEQ, D), 1.0)
    meta = nrm((N_META, D), 1.0)
    norm_mix = 1.0 + nrm((DEPTH, D), 0.02)
    norm_ffn = 1.0 + nrm((DEPTH, D), 0.02)
    norm_final = 1.0 + nrm((D,), 0.02)
    w_in = nrm((DEPTH, D, W_IN), D ** -0.5)
    w_in_vres = nrm((L1, D, A_VRES_LORA), D ** -0.5)
    gate_bias = nrm((DEPTH, GATE_IN), 0.02)
    rwkv_mu = uni((DEPTH, A_IN), 0.0, 1.0)
    rwkv_mu_vres = uni((L1, A_VRES_LORA), 0.0, 1.0)
    rwkv_w0 = uni((DEPTH, A_WIDTH), -6.0, -1.0)
    rwkv_w2 = nrm((DEPTH, A_DECAY_LORA, A_WIDTH), 0.1 * A_DECAY_LORA ** -0.5)
    rwkv_a0 = nrm((DEPTH, A_WIDTH), 0.1)
    rwkv_a2 = nrm((DEPTH, A_ICLR_LORA, A_WIDTH), 0.5 * A_ICLR_LORA ** -0.5)
    rwkv_v0 = 1.0 + nrm((L1, A_WIDTH), 0.1)
    rwkv_v2 = nrm((L1, A_VRES_LORA, A_WIDTH), 0.5 * A_VRES_LORA ** -0.5)
    rwkv_g2 = nrm((DEPTH, A_GATE_LORA, A_WIDTH), A_GATE_LORA ** -0.5)
    rwkv_k_k = 0.85 + nrm((DEPTH, A_WIDTH), 0.02)
    rwkv_k_a = 1.0 + nrm((DEPTH, A_WIDTH), 0.02)
    rwkv_r_k = nrm((DEPTH, A_HEADS, A_HEAD), 0.1)
    rwkv_ln_g = 1.0 + nrm((DEPTH, A_WIDTH), 0.02)
    rwkv_ln_b = nrm((DEPTH, A_WIDTH), 0.02)
    ssm_conv_w = nrm((DEPTH, B_CONV, B_CONV_DIM), B_CONV ** -0.5)
    ssm_conv_b = nrm((DEPTH, B_CONV_DIM), 0.02)
    dt0 = jnp.exp(uni((DEPTH, B_HEADS), math.log(1e-3), math.log(1e-1)))
    ssm_dt_bias = dt0 + jnp.log(-jnp.expm1(-dt0))
    ssm_a_log = jnp.log(uni((DEPTH, B_HEADS), 1.0, 16.0))
    ssm_d = 1.0 + nrm((DEPTH, B_HEADS), 0.1)
    ssm_norm_g = 1.0 + nrm((DEPTH, B_WIDTH), 0.02)
    w_branch = jnp.concatenate([nrm((DEPTH, A_WIDTH, D), A_WIDTH ** -0.5),
                                nrm((DEPTH, B_WIDTH, D), B_WIDTH ** -0.5),
                                nrm((DEPTH, C_WIDTH, D), C_WIDTH ** -0.5)], axis=1)
    w_out = nrm((DEPTH, D, D), D ** -0.5)
    ffn_w_up = nrm((DEPTH, D, 2 * FFN_HIDDEN), D ** -0.5)
    ffn_conv_w = nrm((DEPTH, FFN_CONV, 2 * FFN_HIDDEN), FFN_CONV ** -0.5)
    ffn_conv_b = nrm((DEPTH, 2 * FFN_HIDDEN), 0.02)
    ffn_w_down = nrm((DEPTH, FFN_HIDDEN, D), FFN_HIDDEN ** -0.5)
    return {"x": x, "meta": meta, "norm_mix": norm_mix, "norm_ffn": norm_ffn,
            "norm_final": norm_final, "w_in": w_in, "w_in_vres": w_in_vres,
            "gate_bias": gate_bias, "rwkv_mu": rwkv_mu, "rwkv_mu_vres": rwkv_mu_vres,
            "rwkv_w0": rwkv_w0, "rwkv_w2": rwkv_w2, "rwkv_a0": rwkv_a0, "rwkv_a2": rwkv_a2,
            "rwkv_v0": rwkv_v0, "rwkv_v2": rwkv_v2, "rwkv_g2": rwkv_g2, "rwkv_k_k": rwkv_k_k,
            "rwkv_k_a": rwkv_k_a, "rwkv_r_k": rwkv_r_k, "rwkv_ln_g": rwkv_ln_g,
            "rwkv_ln_b": rwkv_ln_b, "ssm_conv_w": ssm_conv_w, "ssm_conv_b": ssm_conv_b,
            "ssm_dt_bias": ssm_dt_bias, "ssm_a_log": ssm_a_log, "ssm_d": ssm_d,
            "ssm_norm_g": ssm_norm_g, "w_branch": w_branch, "w_out": w_out,
            "ffn_w_up": ffn_w_up, "ffn_conv_w": ffn_conv_w, "ffn_conv_b": ffn_conv_b,
            "ffn_w_down": ffn_w_down}


def reference(x, meta, norm_mix, norm_ffn, norm_final, w_in, w_in_vres, gate_bias,
              rwkv_mu, rwkv_mu_vres, rwkv_w0, rwkv_w2, rwkv_a0, rwkv_a2, rwkv_v0, rwkv_v2,
              rwkv_g2, rwkv_k_k, rwkv_k_a, rwkv_r_k, rwkv_ln_g, rwkv_ln_b,
              ssm_conv_w, ssm_conv_b, ssm_dt_bias, ssm_a_log, ssm_d, ssm_norm_g,
              w_branch, w_out, ffn_w_up, ffn_conv_w, ffn_conv_b, ffn_w_down):
    bsz = x.shape[0]
    h = jnp.concatenate([jnp.broadcast_to(meta.astype(x.dtype), (bsz, N_META, D_MODEL)), x], axis=1)
    L = h.shape[1]
    pos = jnp.arange(L, dtype=jnp.float32)
    inv_freq = 1.0 / (ROPE_BASE ** jnp.linspace(0.0, 1.0, C_QK_HEAD // 2, dtype=jnp.float32))
    ang = pos[:, None] * inv_freq[None, :]
    cos, sin = jnp.cos(ang), jnp.sin(ang)
    v_first = None
    for l in range(DEPTH):
        u = rms_norm(h, norm_mix[l])
        if l == 0:
            w = w_in[0]
        else:
            w = jnp.concatenate([w_in[l], w_in_vres[l - 1]], axis=1)
        proj = u @ w
        p_a, p_b, p_c, p_g = split_last(proj[..., :W_IN], [A_IN, B_IN, C_IN, GATE_IN])
        vres = None if l == 0 else (proj[..., W_IN:], rwkv_mu_vres[l - 1], rwkv_v0[l - 1], rwkv_v2[l - 1])
        y_a, v_first = rwkv7_mix(p_a, rwkv_mu[l], rwkv_w0[l], rwkv_w2[l], rwkv_a0[l], rwkv_a2[l],
                                 rwkv_g2[l], rwkv_k_k[l], rwkv_k_a[l], rwkv_r_k[l],
                                 rwkv_ln_g[l], rwkv_ln_b[l], v_first, vres)
        y_b = mamba2_mix(p_b, ssm_conv_w[l], ssm_conv_b[l], ssm_dt_bias[l], ssm_a_log[l],
                         ssm_d[l], ssm_norm_g[l])
        y_c = retention_mix(p_c, cos, sin)
        gates = jax.nn.sigmoid(p_g + gate_bias[l]).reshape(bsz, L, N_BRANCH, D_MODEL)
        wb = w_branch[l]
        merged = (gates[:, :, 0] * (y_a @ wb[:A_WIDTH])
                  + gates[:, :, 1] * (y_b @ wb[A_WIDTH:A_WIDTH + B_WIDTH])
                  + gates[:, :, 2] * (y_c @ wb[A_WIDTH + B_WIDTH:]))
        h = h + merged @ w_out[l]
        u = rms_norm(h, norm_ffn[l])
        up = causal_dwconv(u @ ffn_w_up[l], ffn_conv_w[l], ffn_conv_b[l])
        gate, val = split_last(up, [FFN_HIDDEN, FFN_HIDDEN])
        h = h + (jax.nn.silu(gate) * val) @ ffn_w_down[l]
    return rms_norm(h[:, N_META:], norm_final)
```

```python
import functools
import math

import numpy as np
import jax
import jax.numpy as jnp
from jax import lax
from jax.experimental import pallas as pl
from jax.experimental.pallas import tpu as pltpu

F32 = jnp.float32
BF16 = jnp.bfloat16

D_MODEL = 1024
CHUNK = 64
N_META = 16
EPS = 1e-6

A_HEAD = 64
A_WIDTH = D_MODEL
A_HEADS = A_WIDTH // A_HEAD
A_DECAY_LORA = 64
A_ICLR_LORA = 64
A_VRES_LORA = 32
A_GATE_LORA = 128
A_LN_EPS = 64e-5
A_IN = 3 * A_WIDTH + A_DECAY_LORA + A_ICLR_LORA + A_GATE_LORA

B_WIDTH = 2 * D_MODEL
B_HEAD = 64
B_HEADS = B_WIDTH // B_HEAD
B_GROUPS = 4
B_HEADS_PER_GROUP = B_HEADS // B_GROUPS
B_STATE = 128
B_CONV = 4
B_CONV_DIM = B_WIDTH + 2 * B_GROUPS * B_STATE
B_IN = B_WIDTH + B_CONV_DIM + B_HEADS

C_HEADS = 8
C_QK_HEAD = 64
C_V_HEAD = 128
C_QK = C_HEADS * C_QK_HEAD
C_WIDTH = C_HEADS * C_V_HEAD
C_IN = 2 * C_QK + 2 * C_WIDTH
ROPE_BASE = 10000.0

GATE_IN = 3 * D_MODEL
FFN_HIDDEN = 2816
FFN_CONV = 3

LANES = 128

COL_XBC = 0
COL_GATE = 3072
COL_Z = 6144
COL_RET_V = 8192
COL_RET_G = 9216
COL_RET_Q = 10240
COL_RET_K = 10752
COL_A_R = 11264
COL_A_K = 12288
COL_A_V = 13312
COL_A_WA = 14336
COL_A_GL = 14464
COL_A_VR = 14592
COL_DT = 14720
N_PROJ = 14848

RWKV_HEADS_PER_STEP = 8
VMEM_LIMIT = 48 * 1024 * 1024

NN = (((1,), (0,)), ((), ()))
NT = (((1,), (1,)), ((), ()))
TN = (((0,), (0,)), ((), ()))


def _dg(a, b, dims):
    return lax.dot_general(a, b, dims, preferred_element_type=F32)


def _split2(x):
    hi = x.astype(BF16)
    lo = (x - hi.astype(F32)).astype(BF16)
    return hi, lo


def _split3(x):
    hi = x.astype(BF16)
    r = x - hi.astype(F32)
    mid = r.astype(BF16)
    lo = (r - mid.astype(F32)).astype(BF16)
    return hi, mid, lo


def _mm(a, b, dims=NN, passes=1):
    if passes == 1:
        return _dg(a.astype(BF16), b.astype(BF16), dims)
    ah, al = _split2(a)
    bh, bl = _split2(b)
    return _dg(ah, bh, dims) + (_dg(ah, bl, dims) + _dg(al, bh, dims))


def _mm_exact_lhs(a_bf16, b, dims=NN):
    h, m, l = _split3(b)
    return _dg(a_bf16, h, dims) + (_dg(a_bf16, m, dims) + _dg(a_bf16, l, dims))


def _mm_exact_rhs(a, b_bf16, dims=NN):
    h, m, l = _split3(a)
    return _dg(h, b_bf16, dims) + (_dg(m, b_bf16, dims) + _dg(l, b_bf16, dims))


def _sigmoid(x):
    return 1.0 / (1.0 + jnp.exp(-x))


def _silu(x):
    return x * _sigmoid(x)


def _softplus(x):
    return jnp.maximum(x, 0.0) + jnp.log(1.0 + jnp.exp(-jnp.abs(x)))


def _shift_rows(x, prev, s, row):
    return jnp.where(row < s, pltpu.roll(prev, s, 0), pltpu.roll(x, s, 0))


def _largest_divisor(n, cap, mult=8):
    best = None
    for d in range(mult, cap + 1, mult):
        if n % d == 0:
            best = d
    assert best is not None
    return best


def _norm_matmul_kernel(h_ref, g_ref, w_ref, o_ref, u_ref, *, tm, lp, pad):
    i = pl.program_id(0)

    @pl.when(pl.program_id(1) == 0)
    def _():
        x = h_ref[...]
        y = x * lax.rsqrt(jnp.mean(x * x, axis=-1, keepdims=True) + EPS) * g_ref[...]
        pos = (i % (lp // tm)) * tm + lax.broadcasted_iota(jnp.int32, (tm, 1), 0)
        u_ref[...] = jnp.where(pos >= pad, y, 0.0).astype(BF16)

    o_ref[...] = jnp.dot(u_ref[...], w_ref[...], preferred_element_type=F32).astype(o_ref.dtype)


def _norm_matmul(h2d, g, w, *, lp, pad, tn, out_dtype):
    m, d = h2d.shape
    n = w.shape[1]
    tm = _largest_divisor(lp, 1056)
    assert m % tm == 0 and n % tn == 0
    return pl.pallas_call(
        functools.partial(_norm_matmul_kernel, tm=tm, lp=lp, pad=pad),
        grid=(m // tm, n // tn),
        in_specs=[pl.BlockSpec((tm, d), lambda i, j: (i, 0)),
                  pl.BlockSpec((1, d), lambda i, j: (0, 0)),
                  pl.BlockSpec((d, tn), lambda i, j: (0, j))],
        out_specs=pl.BlockSpec((tm, tn), lambda i, j: (i, j)),
        out_shape=jax.ShapeDtypeStruct((m, n), out_dtype),
        scratch_shapes=[pltpu.VMEM((tm, d), BF16)],
        compiler_params=pltpu.CompilerParams(dimension_semantics=("parallel", "arbitrary"),
                                             vmem_limit_bytes=VMEM_LIMIT),
        name="norm_matmul",
    )(h2d, g.reshape(1, d), w)


RWKV_PASSES = 3


def _unit_lower_inverse(n, eye, same16, same32):
    p = jnp.where(same16, n, 0.0)
    t = eye + p
    for _ in range(3):
        p = _mm(p, p, NN, RWKV_PASSES)
        t = t + _mm(t, p, NN, RWKV_PASSES)
    off = jnp.where(same32, n, 0.0) - jnp.where(same16, n, 0.0)
    t = t + _mm(_mm(t, off, NN, RWKV_PASSES), t, NN, RWKV_PASSES)
    off = jnp.where(same32, 0.0, n)
    t = t + _mm(_mm(t, off, NN, RWKV_PASSES), t, NN, RWKV_PASSES)
    return t


def _rwkv_kernel(*refs, heads, has_vres):
    it = iter(refs)
    pr_ref, pk_ref, pv_ref, pwa_ref, pgl_ref = (next(it) for _ in range(5))
    if has_vres:
        pvr_ref, vf_ref = next(it), next(it)
    pvec_ref, lmu_ref, wwa_ref, g2_ref = (next(it) for _ in range(4))
    if has_vres:
        v2_ref = next(it)
    y_ref = next(it)
    if not has_vres:
        vfo_ref = next(it)
    prev_r, prev_k, prev_v, prev_wa, prev_gl = (next(it) for _ in range(5))
    if has_vres:
        prev_vr = next(it)
    st_ref = next(it)

    c = pl.program_id(2)
    ck = CHUNK
    gw = heads * A_HEAD

    @pl.when(c == 0)
    def _():
        for ref in (prev_r, prev_k, prev_v, prev_wa, prev_gl) + ((prev_vr,) if has_vres else ()):
            ref[...] = jnp.zeros(ref.shape, F32)
        st_ref[...] = jnp.zeros(st_ref.shape, F32)

    row = lax.broadcasted_iota(jnp.int32, (ck, 1), 0)

    def tshift(x_ref, prev_ref, mu):
        x = x_ref[0]
        prev = _shift_rows(x, prev_ref[...], 1, row)
        prev_ref[...] = x
        return x + (prev - x) * mu

    pvec = pvec_ref[...]
    mu_r, mu_k, mu_v, w0, a0, k_k, k_a, r_k, ln_g, ln_b, v0 = (pvec[i:i + 1] for i in range(11))
    lmu = lmu_ref[...]

    r = tshift(pr_ref, prev_r, mu_r)
    k = tshift(pk_ref, prev_k, mu_k)
    v = tshift(pv_ref, prev_v, mu_v)
    wa = tshift(pwa_ref, prev_wa, lmu[0:1])
    gl = tshift(pgl_ref, prev_gl, lmu[1:2])

    lane = lax.broadcasted_iota(jnp.int32, (1, LANES), 1)
    wa_act = jnp.where(lane < A_DECAY_LORA, jnp.tanh(wa), wa)
    wa_out = _mm(wa_act, wwa_ref[0])
    w_log = -_softplus(-(w0 + wa_out[:, :gw])) - 0.5
    lw = -jnp.exp(w_log)
    a = _sigmoid(a0 + wa_out[:, gw:])
    g = _mm(_sigmoid(gl), g2_ref[...])
    if has_vres:
        vr = tshift(pvr_ref, prev_vr, lmu[2:3])
        v = v + (vf_ref[0] - v) * _sigmoid(v0 + _mm(vr, v2_ref[...]))
    else:
        vfo_ref[0] = v

    ri = lax.broadcasted_iota(jnp.int32, (ck, ck), 0)
    ci = lax.broadcasted_iota(jnp.int32, (ck, ck), 1)
    incl = ri >= ci
    strict = ri > ci
    same16 = (ri // 16) == (ci // 16)
    same32 = (ri // 32) == (ci // 32)
    eye = jnp.where(ri == ci, 1.0, 0.0).astype(F32)
    tril = jnp.where(incl, 1.0, 0.0).astype(BF16)

    cum = _mm_exact_lhs(tril, lw)
    cum_last = cum[ck - 1:ck, :]
    e_inc = jnp.exp(cum)
    e_exc = jnp.exp(cum - lw)
    e_neg = jnp.exp(-cum)
    e_end = jnp.exp(cum_last - cum)
    w_end = jnp.exp(cum_last)

    kk_raw = k * k_k
    kh = k * (1.0 + (a - 1.0) * k_a)
    rt_all = r * e_inc
    at_all = -(kk_raw * e_exc)
    khat_all = kh * e_neg
    bhat_all = kk_raw * a * e_neg
    kt_all = kh * e_end
    bt_all = kk_raw * a * e_end
    rk_all = r * kh * r_k

    outs = []
    p = RWKV_PASSES
    for h in range(heads):
        hs = slice(h * A_HEAD, (h + 1) * A_HEAD)
        kkr = kk_raw[:, hs]
        nrm = lax.rsqrt(jnp.maximum(jnp.sum(kkr * kkr, axis=-1, keepdims=True), 1e-24))
        at = at_all[:, hs] * nrm
        bhat = bhat_all[:, hs] * nrm
        bt = bt_all[:, hs] * nrm
        vh = v[:, hs]
        s0 = st_ref[h]
        ar = jnp.concatenate([at, rt_all[:, hs]], axis=0)
        mk = _mm(ar, khat_all[:, hs], NT, p)
        mb = _mm(ar, bhat, NT, p)
        a_ak = jnp.where(strict, mk[:ck], 0.0)
        a_rk = jnp.where(incl, mk[ck:], 0.0)
        a_ab = jnp.where(strict, mb[:ck], 0.0)
        a_rb = jnp.where(incl, mb[ck:], 0.0)
        tinv = _unit_lower_inverse(a_ab, eye, same16, same32)
        ars = _mm(ar, s0, NT, p)
        u = _mm(tinv, ars[:ck] + _mm(a_ak, vh, NN, p), NN, p)
        y = ars[ck:] + _mm(a_rk, vh, NN, p) + _mm(a_rb, u, NN, p)
        st_ref[h] = s0 * w_end[:, hs] + _mm(vh, kt_all[:, hs], TN, p) + _mm(u, bt, TN, p)

        yc = y - jnp.mean(y, axis=-1, keepdims=True)
        yn = yc * lax.rsqrt(jnp.mean(yc * yc, axis=-1, keepdims=True) + A_LN_EPS)
        bonus = jnp.sum(rk_all[:, hs], axis=-1, keepdims=True) * vh
        outs.append((yn * ln_g[:, hs] + ln_b[:, hs] + bonus) * g[:, hs])
    y_ref[0] = jnp.concatenate(outs, axis=-1).astype(y_ref.dtype)


def _rwkv_mix(proj, v_first, pvec, lmu, wwa, g2, v2, *, has_vres):
    bsz, lp, _ = proj.shape
    heads = RWKV_HEADS_PER_STEP
    gw = heads * A_HEAD
    ngrp = A_WIDTH // gw
    nc = lp // CHUNK

    def col(off, width):
        return pl.BlockSpec((1, CHUNK, width), lambda b, hg, c, o=off // width: (b, c, o))

    def colg(off):
        return pl.BlockSpec((1, CHUNK, gw), lambda b, hg, c, o=off // gw: (b, c, o + hg))

    act_g = pl.BlockSpec((1, CHUNK, gw), lambda b, hg, c: (b, c, hg))
    in_specs = [colg(COL_A_R), colg(COL_A_K), colg(COL_A_V), col(COL_A_WA, LANES), col(COL_A_GL, LANES)]
    args = [proj] * 5
    if has_vres:
        in_specs += [col(COL_A_VR, LANES), act_g]
        args += [proj, v_first]
    in_specs += [pl.BlockSpec((16, gw), lambda b, hg, c: (0, hg)),
                 pl.BlockSpec((8, LANES), lambda b, hg, c: (0, 0)),
                 pl.BlockSpec((1, LANES, 2 * gw), lambda b, hg, c: (hg, 0, 0)),
                 pl.BlockSpec((A_GATE_LORA, gw), lambda b, hg, c: (0, hg))]
    args += [pvec, lmu, wwa, g2]
    if has_vres:
        in_specs.append(pl.BlockSpec((LANES, gw), lambda b, hg, c: (0, hg)))
        args.append(v2)
    y_shape = jax.ShapeDtypeStruct((bsz, lp, A_WIDTH), BF16)
    if has_vres:
        out_shape, out_specs = y_shape, act_g
    else:
        out_shape = (y_shape, jax.ShapeDtypeStruct((bsz, lp, A_WIDTH), F32))
        out_specs = (act_g, act_g)
    scratch = [pltpu.VMEM((CHUNK, gw), F32)] * 3 + [pltpu.VMEM((CHUNK, LANES), F32)] * (3 if has_vres else 2)
    scratch.append(pltpu.VMEM((heads, A_HEAD, A_HEAD), F32))
    out = pl.pallas_call(
        functools.partial(_rwkv_kernel, heads=heads, has_vres=has_vres),
        grid=(bsz, ngrp, nc),
        in_specs=in_specs, out_specs=out_specs, out_shape=out_shape,
        scratch_shapes=scratch,
        compiler_params=pltpu.CompilerParams(dimension_semantics=("parallel", "parallel", "arbitrary"),
                                             vmem_limit_bytes=VMEM_LIMIT),
        name="rwkv7_vres" if has_vres else "rwkv7",
    )(*args)
    return (out, v_first) if has_vres else out


def _ssd_kernel(z_ref, xbc_ref, dt_ref, cw_ref, cb_ref, hp_ref, cp_ref, e_ref, y_ref, prev_ref, st_ref, *, pad):
    c = pl.program_id(1)
    ck = CHUNK
    gp = B_HEADS_PER_GROUP * B_HEAD

    @pl.when(c == 0)
    def _():
        prev_ref[...] = jnp.zeros(prev_ref.shape, F32)
        st_ref[...] = jnp.zeros(st_ref.shape, F32)

    row = lax.broadcasted_iota(jnp.int32, (ck, 1), 0)
    x_raw = xbc_ref[0]
    prev = prev_ref[...]
    cw = cw_ref[...]
    conv = cb_ref[...] + x_raw * cw[B_CONV - 1:B_CONV]
    for s in range(1, B_CONV):
        conv = conv + _shift_rows(x_raw, prev, s, row) * cw[B_CONV - 1 - s:B_CONV - s]
    prev_ref[...] = x_raw
    xa = _silu(conv)
    valid = (c * ck + row) >= pad
    xs = jnp.where(valid, xa[:, :B_WIDTH], 0.0)
    bm = xa[:, B_WIDTH:B_WIDTH + B_GROUPS * B_STATE]
    cm = xa[:, B_WIDTH + B_GROUPS * B_STATE:]

    hp = hp_ref[...]
    dt = _softplus(dt_ref[0] + hp[0:1])
    a = dt * (-jnp.exp(hp[1:2]))

    ri = lax.broadcasted_iota(jnp.int32, (ck, ck), 0)
    ci = lax.broadcasted_iota(jnp.int32, (ck, ck), 1)
    causal = ri >= ci
    tril = jnp.where(causal, 1.0, 0.0).astype(BF16)
    triu = jnp.where(ri <= ci, 1.0, 0.0).astype(BF16)
    a_cs = _mm_exact_lhs(tril, a)
    a_cs_t = _mm_exact_rhs(a, triu, TN)
    e = e_ref[...]
    dt_e = _mm_exact_rhs(dt, e)
    acs_e = _mm_exact_rhs(a_cs, e)
    atot_e = acs_e[ck - 1:ck, :]
    xdt = xs * dt_e
    xw = xdt * jnp.exp(atot_e - acs_e)
    e_in = jnp.exp(acs_e)
    e_tot = jnp.exp(atot_e)

    cp = cp_ref[...]
    d_skip, norm_g = cp[0:1], cp[1:2]
    z = z_ref[0]
    outs = []
    for g in range(B_GROUPS):
        gs = slice(g * gp, (g + 1) * gp)
        bg = bm[:, g * B_STATE:(g + 1) * B_STATE]
        cg = cm[:, g * B_STATE:(g + 1) * B_STATE]
        cb = _mm(cg, bg, NT)
        st = st_ref[:, gs]
        y_off = _mm(cg, st) * e_in[:, gs]
        y_diag = []
        for r in range(B_HEADS_PER_GROUP):
            h = g * B_HEADS_PER_GROUP + r
            hs = slice(h * B_HEAD, (h + 1) * B_HEAD)
            seg = acs_e[:, hs] - a_cs_t[h:h + 1, :]
            lmat = jnp.exp(jnp.where(causal, seg, -1e30))
            y_diag.append(_mm(cb * lmat, xdt[:, hs]))
        st_ref[:, gs] = st * e_tot[:, gs] + _mm(bg, xw[:, gs], TN)
        y = jnp.concatenate(y_diag, axis=-1) + y_off
        y = (y + xs[:, gs] * d_skip[:, gs]) * _silu(z[:, gs])
        y = y * lax.rsqrt(jnp.mean(y * y, axis=-1, keepdims=True) + EPS)
        outs.append(y * norm_g[:, gs])
    y_ref[0] = jnp.concatenate(outs, axis=-1).astype(y_ref.dtype)


def _ssd_mix(proj, conv_w, conv_b, head_params, chan_params, expand, *, pad):
    bsz, lp, _ = proj.shape
    nc = lp // CHUNK

    def col(off, width):
        return pl.BlockSpec((1, CHUNK, width), lambda b, c, o=off // width: (b, c, o))

    def full(shape):
        return pl.BlockSpec(shape, lambda b, c: (0,) * len(shape))

    return pl.pallas_call(
        functools.partial(_ssd_kernel, pad=pad),
        grid=(bsz, nc),
        in_specs=[col(COL_Z, B_WIDTH), col(COL_XBC, B_CONV_DIM), col(COL_DT, LANES),
                  full((8, B_CONV_DIM)), full((1, B_CONV_DIM)), full((8, LANES)), full((8, B_WIDTH)),
                  full((LANES, B_WIDTH))],
        out_specs=pl.BlockSpec((1, CHUNK, B_WIDTH), lambda b, c: (b, c, 0)),
        out_shape=jax.ShapeDtypeStruct((bsz, lp, B_WIDTH), BF16),
        scratch_shapes=[pltpu.VMEM((CHUNK, B_CONV_DIM), F32), pltpu.VMEM((B_STATE, B_WIDTH), F32)],
        compiler_params=pltpu.CompilerParams(dimension_semantics=("parallel", "arbitrary"),
                                             vmem_limit_bytes=VMEM_LIMIT),
        name="ssd",
    )(proj, proj, proj, conv_w, conv_b, head_params, chan_params, expand)


def _retention_kernel(q_ref, k_ref, v_ref, g_ref, cos_ref, sin_ref, y_ref, st_ref):
    c = pl.program_id(1)
    ck = CHUNK

    @pl.when(c == 0)
    def _():
        st_ref[...] = jnp.zeros(st_ref.shape, F32)

    cos_t = cos_ref[...]
    sin_t = sin_ref[...]
    even = (lax.broadcasted_iota(jnp.int32, (1, C_QK), 1) % 2) == 0

    def rot(t):
        partner = jnp.where(even, pltpu.roll(t, C_QK - 1, 1), pltpu.roll(t, 1, 1))
        return t * cos_t + partner * sin_t

    q = rot(q_ref[0])
    k = rot(k_ref[0]) * (C_QK_HEAD ** -0.5)
    v = v_ref[0]
    gate = _silu(g_ref[0])

    ri = lax.broadcasted_iota(jnp.int32, (ck, ck), 0)
    ci = lax.broadcasted_iota(jnp.int32, (ck, ck), 1)
    rel = (ri - ci).astype(F32)
    rowf = ri.astype(F32)
    outs = []
    for h in range(C_HEADS):
        log_gamma = math.log(1.0 - 2.0 ** (-5.0 - h))
        qs = slice(h * C_QK_HEAD, (h + 1) * C_QK_HEAD)
        vs = slice(h * C_V_HEAD, (h + 1) * C_V_HEAD)
        qh, kh, vh = q[:, qs], k[:, qs], v[:, vs]
        d_intra = jnp.where(rel >= 0, jnp.exp(jnp.maximum(rel, 0.0) * log_gamma), 0.0)
        q_decay = jnp.exp((rowf + 1.0) * log_gamma)
        k_decay = jnp.exp((ck - 1.0 - rowf) * log_gamma)
        s0 = st_ref[h]
        y = _mm(_mm(qh, kh, NT) * d_intra, vh) + _mm(qh * q_decay, s0)
        st_ref[h] = s0 * math.exp(ck * log_gamma) + _mm(kh * k_decay, vh, TN)
        yc = y - jnp.mean(y, axis=-1, keepdims=True)
        outs.append(yc * lax.rsqrt(jnp.mean(yc * yc, axis=-1, keepdims=True) + EPS))
    y_ref[0] = (gate * jnp.concatenate(outs, axis=-1)).astype(y_ref.dtype)


def _retention_mix(proj, cos_t, sin_t):
    bsz, lp, _ = proj.shape
    nc = lp // CHUNK

    def col(off, width):
        return pl.BlockSpec((1, CHUNK, width), lambda b, c, o=off // width: (b, c, o))

    tab = pl.BlockSpec((CHUNK, C_QK), lambda b, c: (c, 0))
    return pl.pallas_call(
        _retention_kernel,
        grid=(bsz, nc),
        in_specs=[col(COL_RET_Q, C_QK), col(COL_RET_K, C_QK), col(COL_RET_V, C_WIDTH),
                  col(COL_RET_G, C_WIDTH), tab, tab],
        out_specs=pl.BlockSpec((1, CHUNK, C_WIDTH), lambda b, c: (b, c, 0)),
        out_shape=jax.ShapeDtypeStruct((bsz, lp, C_WIDTH), BF16),
        scratch_shapes=[pltpu.VMEM((C_HEADS, C_QK_HEAD, C_V_HEAD), F32)],
        compiler_params=pltpu.CompilerParams(dimension_semantics=("parallel", "arbitrary"),
                                             vmem_limit_bytes=VMEM_LIMIT),
        name="retention",
    )(proj, proj, proj, proj, cos_t, sin_t)


def _merge_kernel(ya_ref, yb_ref, yc_ref, g0_ref, g1_ref, g2_ref, gb_ref, h_ref,
                  wa_ref, wb_ref, wc_ref, wo_ref, o_ref):
    gb = gb_ref[...]
    d = D_MODEL
    merged = (_sigmoid(g0_ref[...] + gb[:, :d]) * _dg(ya_ref[...], wa_ref[...], NN)
              + _sigmoid(g1_ref[...] + gb[:, d:2 * d]) * _dg(yb_ref[...], wb_ref[...], NN)
              + _sigmoid(g2_ref[...] + gb[:, 2 * d:]) * _dg(yc_ref[...], wc_ref[...], NN))
    o_ref[...] = h_ref[...] + _dg(merged.astype(BF16), wo_ref[...], NN)


def _merge(ya, yb, yc, proj2d, gate_bias, h2d, wba, wbb, wbc, wo):
    m, d = h2d.shape
    tm = _largest_divisor(m, 352)

    def rows(width):
        return pl.BlockSpec((tm, width), lambda i: (i, 0))

    def gcol(j):
        return pl.BlockSpec((tm, d), lambda i, o=COL_GATE // d + j: (i, o))

    def full(shape):
        return pl.BlockSpec(shape, lambda i: (0, 0))

    return pl.pallas_call(
        _merge_kernel,
        grid=(m // tm,),
        in_specs=[rows(A_WIDTH), rows(B_WIDTH), rows(C_WIDTH), gcol(0), gcol(1), gcol(2),
                  full((1, GATE_IN)), rows(d),
                  full((A_WIDTH, d)), full((B_WIDTH, d)), full((C_WIDTH, d)), full((d, d))],
        out_specs=rows(d),
        out_shape=jax.ShapeDtypeStruct((m, d), F32),
        compiler_params=pltpu.CompilerParams(dimension_semantics=("parallel",),
                                             vmem_limit_bytes=VMEM_LIMIT),
        name="merge",
    )(ya, yb, yc, proj2d, proj2d, proj2d, gate_bias, h2d, wba, wbb, wbc, wo)


def _ffn_act_kernel(ug_ref, uv_ref, wg_ref, wv_ref, bg_ref, bv_ref, o_ref, pg_ref, pv_ref, *, tr):
    @pl.when(pl.program_id(2) == 0)
    def _():
        pg_ref[...] = jnp.zeros(pg_ref.shape, F32)
        pv_ref[...] = jnp.zeros(pv_ref.shape, F32)

    row = lax.broadcasted_iota(jnp.int32, (tr, 1), 0)

    def conv(x_ref, prev_ref, w_ref, b_ref):
        x = x_ref[0]
        prev = prev_ref[...]
        w = w_ref[...]
        out = b_ref[...] + x * w[FFN_CONV - 1:FFN_CONV]
        for s in range(1, FFN_CONV):
            out = out + _shift_rows(x, prev, s, row) * w[FFN_CONV - 1 - s:FFN_CONV - s]
        prev_ref[...] = x
        return out

    gate = conv(ug_ref, pg_ref, wg_ref, bg_ref)
    val = conv(uv_ref, pv_ref, wv_ref, bv_ref)
    o_ref[0] = (_silu(gate) * val).astype(o_ref.dtype)


def _ffn_act(up, conv_w, conv_b):
    bsz, lp, _ = up.shape
    tc = FFN_HIDDEN // 2
    ncol = FFN_HIDDEN // tc
    tr = _largest_divisor(lp, 704)

    def blk(off):
        return pl.BlockSpec((1, tr, tc), lambda b, j, i, o=off: (b, i, j + o))

    def par(rows, off):
        return pl.BlockSpec((rows, tc), lambda b, j, i, o=off: (0, j + o))

    return pl.pallas_call(
        functools.partial(_ffn_act_kernel, tr=tr),
        grid=(bsz, ncol, lp // tr),
        in_specs=[blk(0), blk(ncol), par(8, 0), par(8, ncol), par(1, 0), par(1, ncol)],
        out_specs=pl.BlockSpec((1, tr, tc), lambda b, j, i: (b, i, j)),
        out_shape=jax.ShapeDtypeStruct((bsz, lp, FFN_HIDDEN), BF16),
        scratch_shapes=[pltpu.VMEM((tr, tc), F32), pltpu.VMEM((tr, tc), F32)],
        compiler_params=pltpu.CompilerParams(dimension_semantics=("parallel", "parallel", "arbitrary"),
                                             vmem_limit_bytes=VMEM_LIMIT),
        name="ffn_conv_gate",
    )(up, up, conv_w, conv_w, conv_b, conv_b)


def _matmul_residual_kernel(a_ref, w_ref, h_ref, o_ref):
    o_ref[...] = h_ref[...] + _dg(a_ref[...], w_ref[...], NN)


def _matmul_residual(a, w, h2d):
    m, d = h2d.shape
    kdim = a.shape[1]
    tm = _largest_divisor(m, 1056)
    return pl.pallas_call(
        _matmul_residual_kernel,
        grid=(m // tm,),
        in_specs=[pl.BlockSpec((tm, kdim), lambda i: (i, 0)), pl.BlockSpec((kdim, d), lambda i: (0, 0)),
                  pl.BlockSpec((tm, d), lambda i: (i, 0))],
        out_specs=pl.BlockSpec((tm, d), lambda i: (i, 0)),
        out_shape=jax.ShapeDtypeStruct((m, d), F32),
        compiler_params=pltpu.CompilerParams(dimension_semantics=("parallel",),
                                             vmem_limit_bytes=VMEM_LIMIT),
        name="ffn_down",
    )(a, w, h2d)


def _final_norm_kernel(h_ref, g_ref, o_ref):
    x = h_ref[0]
    o_ref[0] = x * lax.rsqrt(jnp.mean(x * x, axis=-1, keepdims=True) + EPS) * g_ref[...]


def _final_norm(h3d, g, *, skip_chunks, seq):
    bsz, _, d = h3d.shape
    return pl.pallas_call(
        _final_norm_kernel,
        grid=(bsz, seq // CHUNK),
        in_specs=[pl.BlockSpec((1, CHUNK, d), lambda b, i: (b, i + skip_chunks, 0)),
                  pl.BlockSpec((1, d), lambda b, i: (0, 0))],
        out_specs=pl.BlockSpec((1, CHUNK, d), lambda b, i: (b, i, 0)),
        out_shape=jax.ShapeDtypeStruct((bsz, seq, d), F32),
        compiler_params=pltpu.CompilerParams(dimension_semantics=("parallel", "parallel")),
        name="final_norm",
    )(h3d, g.reshape(1, d))


def _pad_cols(t, width):
    return jnp.pad(t, [(0, 0)] * (t.ndim - 1) + [(0, width - t.shape[-1])])


def _pad_rows(t, rows):
    return jnp.pad(t, [(0, rows - t.shape[0])] + [(0, 0)] * (t.ndim - 1))


def _pack_in_proj(w, w_vres):
    a, b, c, g = (w[:, :A_IN], w[:, A_IN:A_IN + B_IN], w[:, A_IN + B_IN:A_IN + B_IN + C_IN],
                  w[:, A_IN + B_IN + C_IN:])
    aw = 3 * A_WIDTH
    parts = [b[:, B_WIDTH:B_WIDTH + B_CONV_DIM], g, b[:, :B_WIDTH],
             c[:, 2 * C_QK:2 * C_QK + C_WIDTH], c[:, 2 * C_QK + C_WIDTH:], c[:, :C_QK], c[:, C_QK:2 * C_QK],
             a[:, :aw], a[:, aw:aw + LANES], a[:, aw + LANES:],
             _pad_cols(w_vres, LANES), _pad_cols(b[:, B_WIDTH + B_CONV_DIM:], LANES)]
    out = jnp.concatenate(parts, axis=1)
    assert out.shape[1] == N_PROJ
    return out.astype(BF16)


def kernel(x, meta, norm_mix, norm_ffn, norm_final, w_in, w_in_vres, gate_bias, rwkv_mu, rwkv_mu_vres, rwkv_w0, rwkv_w2, rwkv_a0, rwkv_a2, rwkv_v0, rwkv_v2, rwkv_g2, rwkv_k_k, rwkv_k_a, rwkv_r_k, rwkv_ln_g, rwkv_ln_b, ssm_conv_w, ssm_conv_b, ssm_dt_bias, ssm_a_log, ssm_d, ssm_norm_g, w_branch, w_out, ffn_w_up, ffn_conv_w, ffn_conv_b, ffn_w_down):
    bsz, seq, d = x.shape
    depth = w_in.shape[0]
    length = N_META + seq
    pad = (-length) % CHUNK
    lp = pad + length
    assert (pad + N_META) % CHUNK == 0 and seq % CHUNK == 0
    m = bsz * lp

    h = jnp.concatenate([jnp.zeros((bsz, pad, d), F32),
                         jnp.broadcast_to(meta.astype(F32), (bsz, N_META, d)), x.astype(F32)], axis=1)
    h = h.reshape(m, d)

    pos = jnp.arange(length, dtype=F32)
    inv_freq = 1.0 / (ROPE_BASE ** jnp.linspace(0.0, 1.0, C_QK_HEAD // 2, dtype=F32))
    ang = pos[:, None] * inv_freq[None, :]
    cos_t = jnp.tile(jnp.repeat(jnp.cos(ang), 2, axis=1), (1, C_HEADS))
    sin_t = jnp.tile((jnp.repeat(jnp.sin(ang), 2, axis=1)
                      * jnp.tile(jnp.array([-1.0, 1.0], F32), C_QK_HEAD // 2)), (1, C_HEADS))
    cos_t = jnp.pad(cos_t, ((pad, 0), (0, 0)))
    sin_t = jnp.pad(sin_t, ((pad, 0), (0, 0)))

    expand = (jnp.arange(LANES)[:, None] == (jnp.arange(B_WIDTH)[None, :] // B_HEAD)).astype(BF16)

    heads = RWKV_HEADS_PER_STEP
    gw = heads * A_HEAD
    ngrp = A_WIDTH // gw
    aw = 3 * A_WIDTH
    v_first = None
    for l in range(depth):
        has_vres = l > 0
        w_vres = w_in_vres[l - 1] if has_vres else jnp.zeros((d, A_VRES_LORA), F32)
        proj = _norm_matmul(h, norm_mix[l], _pack_in_proj(w_in[l], w_vres), lp=lp, pad=pad, tn=512,
                            out_dtype=F32)
        proj3 = proj.reshape(bsz, lp, N_PROJ)

        mu = rwkv_mu[l]
        zeros_w = jnp.zeros((A_WIDTH,), F32)
        pvec = jnp.stack([mu[:A_WIDTH], mu[A_WIDTH:2 * A_WIDTH], mu[2 * A_WIDTH:aw],
                          rwkv_w0[l], rwkv_a0[l], rwkv_k_k[l], rwkv_k_a[l], rwkv_r_k[l].reshape(-1),
                          rwkv_ln_g[l], rwkv_ln_b[l], rwkv_v0[l - 1] if has_vres else zeros_w]
                         + [zeros_w] * 5)
        mu_vr = _pad_cols(rwkv_mu_vres[l - 1], LANES) if has_vres else jnp.zeros((LANES,), F32)
        lmu = _pad_rows(jnp.stack([mu[aw:aw + LANES], mu[aw + LANES:], mu_vr]), 8)
        w2g = rwkv_w2[l].reshape(A_DECAY_LORA, ngrp, gw).transpose(1, 0, 2)
        a2g = rwkv_a2[l].reshape(A_ICLR_LORA, ngrp, gw).transpose(1, 0, 2)
        wwa = jnp.concatenate([jnp.concatenate([w2g, jnp.zeros_like(w2g)], axis=2),
                               jnp.concatenate([jnp.zeros_like(a2g), a2g], axis=2)], axis=1).astype(BF16)
        v2 = _pad_rows(rwkv_v2[l - 1], LANES).astype(BF16) if has_vres else None
        y_a, v_first = _rwkv_mix(proj3, v_first, pvec, lmu, wwa, rwkv_g2[l].astype(BF16), v2,
                                 has_vres=has_vres)

        head_params = _pad_rows(jnp.stack([_pad_cols(ssm_dt_bias[l], LANES), _pad_cols(ssm_a_log[l], LANES)]), 8)
        chan_params = _pad_rows(jnp.stack([jnp.repeat(ssm_d[l], B_HEAD), ssm_norm_g[l]]), 8)
        y_b = _ssd_mix(proj3, _pad_rows(ssm_conv_w[l], 8), ssm_conv_b[l].reshape(1, -1), head_params,
                       chan_params, expand, pad=pad)

        y_c = _retention_mix(proj3, cos_t, sin_t)

        wb = w_branch[l].astype(BF16)
        h = _merge(y_a.reshape(m, A_WIDTH), y_b.reshape(m, B_WIDTH), y_c.reshape(m, C_WIDTH), proj,
                   gate_bias[l].reshape(1, -1), h, wb[:A_WIDTH], wb[A_WIDTH:A_WIDTH + B_WIDTH],
                   wb[A_WIDTH + B_WIDTH:], w_out[l].astype(BF16))

        up = _norm_matmul(h, norm_ffn[l], ffn_w_up[l].astype(BF16), lp=lp, pad=pad, tn=512, out_dtype=F32)
        act = _ffn_act(up.reshape(bsz, lp, 2 * FFN_HIDDEN), _pad_rows(ffn_conv_w[l], 8),
                       ffn_conv_b[l].reshape(1, -1))
        h = _matmul_residual(act.reshape(m, FFN_HIDDEN), ffn_w_down[l].astype(BF16), h)

    return _final_norm(h.reshape(bsz, lp, d), norm_final, skip_chunks=(pad + N_META) // CHUNK, seq=seq)
```

```python
import functools
import math

import numpy as np
import jax
import jax.numpy as jnp
from jax import lax
from jax.experimental import pallas as pl
from jax.experimental.pallas import tpu as pltpu

F32 = jnp.float32
BF16 = jnp.bfloat16

D_MODEL = 1024
CHUNK = 64
N_META = 16
EPS = 1e-6

A_HEAD = 64
A_WIDTH = D_MODEL
A_HEADS = A_WIDTH // A_HEAD
A_DECAY_LORA = 64
A_ICLR_LORA = 64
A_VRES_LORA = 32
A_GATE_LORA = 128
A_LN_EPS = 64e-5
A_IN = 3 * A_WIDTH + A_DECAY_LORA + A_ICLR_LORA + A_GATE_LORA

B_WIDTH = 2 * D_MODEL
B_HEAD = 64
B_HEADS = B_WIDTH // B_HEAD
B_GROUPS = 4
B_HEADS_PER_GROUP = B_HEADS // B_GROUPS
B_STATE = 128
B_CONV = 4
B_CONV_DIM = B_WIDTH + 2 * B_GROUPS * B_STATE
B_IN = B_WIDTH + B_CONV_DIM + B_HEADS

C_HEADS = 8
C_QK_HEAD = 64
C_V_HEAD = 128
C_QK = C_HEADS * C_QK_HEAD
C_WIDTH = C_HEADS * C_V_HEAD
C_IN = 2 * C_QK + 2 * C_WIDTH
ROPE_BASE = 10000.0

GATE_IN = 3 * D_MODEL
FFN_HIDDEN = 2816
FFN_CONV = 3

LANES = 128

COL_XBC = 0
COL_GATE = 3072
COL_Z = 6144
COL_RET_V = 8192
COL_RET_G = 9216
COL_RET_Q = 10240
COL_RET_K = 10752
COL_A_R = 11264
COL_A_K = 12288
COL_A_V = 13312
COL_A_WA = 14336
COL_A_GL = 14464
COL_A_VR = 14592
COL_DT = 14720
N_PROJ = 14848

RWKV_HEADS_PER_STEP = 16
VMEM_LIMIT = 48 * 1024 * 1024

NN = (((1,), (0,)), ((), ()))
NT = (((1,), (1,)), ((), ()))
TN = (((0,), (0,)), ((), ()))


def _dg(a, b, dims):
    return lax.dot_general(a, b, dims, preferred_element_type=F32)


def _split3(x):
    hi = x.astype(BF16)
    r = x - hi.astype(F32)
    mid = r.astype(BF16)
    lo = (r - mid.astype(F32)).astype(BF16)
    return hi, mid, lo


def _mm(a, b, dims=NN):
    return _dg(a.astype(BF16), b.astype(BF16), dims)


def _mm_exact_lhs(a_bf16, b, dims=NN):
    h, m, l = _split3(b)
    return _dg(a_bf16, h, dims) + (_dg(a_bf16, m, dims) + _dg(a_bf16, l, dims))


def _mm_exact_rhs(a, b_bf16, dims=NN):
    h, m, l = _split3(a)
    return _dg(h, b_bf16, dims) + (_dg(m, b_bf16, dims) + _dg(l, b_bf16, dims))


def _sigmoid(x):
    return 1.0 / (1.0 + jnp.exp(-x))


def _silu(x):
    return x * _sigmoid(x)


def _softplus(x):
    return jnp.maximum(x, 0.0) + jnp.log(1.0 + jnp.exp(-jnp.abs(x)))


def _shift_rows(x, prev, s, row):
    return jnp.where(row < s, pltpu.roll(prev, s, 0), pltpu.roll(x, s, 0))


def _largest_divisor(n, cap, mult=8):
    best = None
    for d in range(mult, cap + 1, mult):
        if n % d == 0:
            best = d
    assert best is not None
    return best


def _norm_matmul_kernel(h_ref, g_ref, w_ref, o_ref, u_ref, *, tm, lp, pad):
    i = pl.program_id(0)

    @pl.when(pl.program_id(1) == 0)
    def _():
        x = h_ref[...]
        y = x * lax.rsqrt(jnp.mean(x * x, axis=-1, keepdims=True) + EPS) * g_ref[...]
        pos = (i % (lp // tm)) * tm + lax.broadcasted_iota(jnp.int32, (tm, 1), 0)
        u_ref[...] = jnp.where(pos >= pad, y, 0.0).astype(BF16)

    o_ref[...] = jnp.dot(u_ref[...], w_ref[...], preferred_element_type=F32).astype(o_ref.dtype)


def _norm_matmul(h2d, g, w, *, lp, pad, tn, out_dtype):
    m, d = h2d.shape
    n = w.shape[1]
    tm = _largest_divisor(lp, 1056)
    assert m % tm == 0 and n % tn == 0
    return pl.pallas_call(
        functools.partial(_norm_matmul_kernel, tm=tm, lp=lp, pad=pad),
        grid=(m // tm, n // tn),
        in_specs=[pl.BlockSpec((tm, d), lambda i, j: (i, 0)),
                  pl.BlockSpec((1, d), lambda i, j: (0, 0)),
                  pl.BlockSpec((d, tn), lambda i, j: (0, j))],
        out_specs=pl.BlockSpec((tm, tn), lambda i, j: (i, j)),
        out_shape=jax.ShapeDtypeStruct((m, n), out_dtype),
        scratch_shapes=[pltpu.VMEM((tm, d), BF16)],
        compiler_params=pltpu.CompilerParams(dimension_semantics=("parallel", "arbitrary"),
                                             vmem_limit_bytes=VMEM_LIMIT),
        name="norm_matmul",
    )(h2d, g.reshape(1, d), w)


def _unit_lower_inverse(ns, eye, same16, same32):
    ps = [jnp.where(same16, n, 0.0) for n in ns]
    ts = [eye + p for p in ps]
    for _ in range(3):
        ps = [_mm(p, p) for p in ps]
        ts = [t + _mm(t, p) for t, p in zip(ts, ps)]
    offs = [jnp.where(same32, n, 0.0) - jnp.where(same16, n, 0.0) for n in ns]
    tmp = [_mm(t, o) for t, o in zip(ts, offs)]
    ts = [t + _mm(x, t) for t, x in zip(ts, tmp)]
    offs = [jnp.where(same32, 0.0, n) for n in ns]
    tmp = [_mm(t, o) for t, o in zip(ts, offs)]
    return [t + _mm(x, t) for t, x in zip(ts, tmp)]


def _rwkv_kernel(*refs, heads, has_vres):
    it = iter(refs)
    pr_ref, pk_ref, pv_ref, pwa_ref, pgl_ref = (next(it) for _ in range(5))
    if has_vres:
        pvr_ref, vf_ref = next(it), next(it)
    pvec_ref, lmu_ref, wwa_ref, g2_ref = (next(it) for _ in range(4))
    if has_vres:
        v2_ref = next(it)
    y_ref = next(it)
    if not has_vres:
        vfo_ref = next(it)
    prev_r, prev_k, prev_v, prev_wa, prev_gl = (next(it) for _ in range(5))
    if has_vres:
        prev_vr = next(it)
    st_ref = next(it)

    c = pl.program_id(2)
    ck = CHUNK
    gw = heads * A_HEAD

    @pl.when(c == 0)
    def _():
        for ref in (prev_r, prev_k, prev_v, prev_wa, prev_gl) + ((prev_vr,) if has_vres else ()):
            ref[...] = jnp.zeros(ref.shape, F32)
        st_ref[...] = jnp.zeros(st_ref.shape, F32)

    row = lax.broadcasted_iota(jnp.int32, (ck, 1), 0)

    def tshift(x_ref, prev_ref, mu):
        x = x_ref[0].astype(F32)
        prev = _shift_rows(x, prev_ref[...], 1, row)
        prev_ref[...] = x
        return x + (prev - x) * mu

    pvec = pvec_ref[...]
    mu_r, mu_k, mu_v, w0, a0, k_k, k_a, r_k, ln_g, ln_b, v0 = (pvec[i:i + 1] for i in range(11))
    lmu = lmu_ref[...]

    r = tshift(pr_ref, prev_r, mu_r)
    k = tshift(pk_ref, prev_k, mu_k)
    v = tshift(pv_ref, prev_v, mu_v)
    wa = tshift(pwa_ref, prev_wa, lmu[0:1])
    gl = tshift(pgl_ref, prev_gl, lmu[1:2])

    lane = lax.broadcasted_iota(jnp.int32, (1, LANES), 1)
    wa_act = jnp.where(lane < A_DECAY_LORA, jnp.tanh(wa), wa)
    wa_out = _mm(wa_act, wwa_ref[0])
    w_log = -_softplus(-(w0 + wa_out[:, :gw])) - 0.5
    lw = -jnp.exp(w_log)
    a = _sigmoid(a0 + wa_out[:, gw:])
    g = _mm(_sigmoid(gl), g2_ref[...])
    if has_vres:
        vr = tshift(pvr_ref, prev_vr, lmu[2:3])
        v = v + (vf_ref[0] - v) * _sigmoid(v0 + _mm(vr, v2_ref[...]))
    else:
        vfo_ref[0] = v

    ri = lax.broadcasted_iota(jnp.int32, (ck, ck), 0)
    ci = lax.broadcasted_iota(jnp.int32, (ck, ck), 1)
    incl = ri >= ci
    strict = ri > ci
    same16 = (ri // 16) == (ci // 16)
    same32 = (ri // 32) == (ci // 32)
    eye = jnp.where(ri == ci, 1.0, 0.0).astype(F32)
    tril = jnp.where(incl, 1.0, 0.0).astype(BF16)

    cum = _mm_exact_lhs(tril, lw)
    cum_last = cum[ck - 1:ck, :]
    e_inc = jnp.exp(cum)
    e_exc = jnp.exp(cum - lw)
    e_neg = jnp.exp(-cum)
    e_end = jnp.exp(cum_last - cum)
    w_end = jnp.exp(cum_last)

    kk_raw = k * k_k
    kh = k * (1.0 + (a - 1.0) * k_a)
    rt_all = r * e_inc
    at_all = -(kk_raw * e_exc)
    khat_all = kh * e_neg
    bhat_all = kk_raw * a * e_neg
    kt_all = kh * e_end
    bt_all = kk_raw * a * e_end
    rk_all = r * kh * r_k

    hsl = [slice(h * A_HEAD, (h + 1) * A_HEAD) for h in range(heads)]
    nrm = [lax.rsqrt(jnp.maximum(jnp.sum(kk_raw[:, hs] * kk_raw[:, hs], axis=-1, keepdims=True), 1e-24))
           for hs in hsl]
    at = [at_all[:, hs] * n for hs, n in zip(hsl, nrm)]
    bhat = [bhat_all[:, hs] * n for hs, n in zip(hsl, nrm)]
    bt = [bt_all[:, hs] * n for hs, n in zip(hsl, nrm)]
    vh = [v[:, hs] for hs in hsl]
    s0 = [st_ref[h] for h in range(heads)]
    ar = [jnp.concatenate([x, rt_all[:, hs]], axis=0) for x, hs in zip(at, hsl)]
    mk = [_mm(x, khat_all[:, hs], NT) for x, hs in zip(ar, hsl)]
    mb = [_mm(x, y, NT) for x, y in zip(ar, bhat)]
    a_ak = [jnp.where(strict, x[:ck], 0.0) for x in mk]
    a_rk = [jnp.where(incl, x[ck:], 0.0) for x in mk]
    a_ab = [jnp.where(strict, x[:ck], 0.0) for x in mb]
    a_rb = [jnp.where(incl, x[ck:], 0.0) for x in mb]
    tinv = _unit_lower_inverse(a_ab, eye, same16, same32)
    ars = [_mm(x, s, NT) for x, s in zip(ar, s0)]
    rhs = [x[:ck] + _mm(m_, v_) for x, m_, v_ in zip(ars, a_ak, vh)]
    u = [_mm(t, x) for t, x in zip(tinv, rhs)]
    ys = [x[ck:] + _mm(m_, v_) + _mm(n_, u_) for x, m_, v_, n_, u_ in zip(ars, a_rk, vh, a_rb, u)]
    for h in range(heads):
        st_ref[h] = s0[h] * w_end[:, hsl[h]] + _mm(vh[h], kt_all[:, hsl[h]], TN) + _mm(u[h], bt[h], TN)
    outs = []
    for h in range(heads):
        hs = hsl[h]
        y = ys[h]
        yc = y - jnp.mean(y, axis=-1, keepdims=True)
        yn = yc * lax.rsqrt(jnp.mean(yc * yc, axis=-1, keepdims=True) + A_LN_EPS)
        bonus = jnp.sum(rk_all[:, hs], axis=-1, keepdims=True) * vh[h]
        outs.append((yn * ln_g[:, hs] + ln_b[:, hs] + bonus) * g[:, hs])
    y_ref[0] = jnp.concatenate(outs, axis=-1).astype(y_ref.dtype)


def _rwkv_mix(proj, v_first, pvec, lmu, wwa, g2, v2, *, has_vres):
    bsz, lp, _ = proj.shape
    heads = RWKV_HEADS_PER_STEP
    gw = heads * A_HEAD
    ngrp = A_WIDTH // gw
    nc = lp // CHUNK

    def col(off, width):
        return pl.BlockSpec((1, CHUNK, width), lambda b, hg, c, o=off // width: (b, c, o))

    def colg(off):
        return pl.BlockSpec((1, CHUNK, gw), lambda b, hg, c, o=off // gw: (b, c, o + hg))

    act_g = pl.BlockSpec((1, CHUNK, gw), lambda b, hg, c: (b, c, hg))
    in_specs = [colg(COL_A_R), colg(COL_A_K), colg(COL_A_V), col(COL_A_WA, LANES), col(COL_A_GL, LANES)]
    args = [proj] * 5
    if has_vres:
        in_specs += [col(COL_A_VR, LANES), act_g]
        args += [proj, v_first]
    in_specs += [pl.BlockSpec((16, gw), lambda b, hg, c: (0, hg)),
                 pl.BlockSpec((8, LANES), lambda b, hg, c: (0, 0)),
                 pl.BlockSpec((1, LANES, 2 * gw), lambda b, hg, c: (hg, 0, 0)),
                 pl.BlockSpec((A_GATE_LORA, gw), lambda b, hg, c: (0, hg))]
    args += [pvec, lmu, wwa, g2]
    if has_vres:
        in_specs.append(pl.BlockSpec((LANES, gw), lambda b, hg, c: (0, hg)))
        args.append(v2)
    y_shape = jax.ShapeDtypeStruct((bsz, lp, A_WIDTH), BF16)
    if has_vres:
        out_shape, out_specs = y_shape, act_g
    else:
        out_shape = (y_shape, jax.ShapeDtypeStruct((bsz, lp, A_WIDTH), F32))
        out_specs = (act_g, act_g)
    scratch = [pltpu.VMEM((CHUNK, gw), F32)] * 3 + [pltpu.VMEM((CHUNK, LANES), F32)] * (3 if has_vres else 2)
    scratch.append(pltpu.VMEM((heads, A_HEAD, A_HEAD), F32))
    out = pl.pallas_call(
        functools.partial(_rwkv_kernel, heads=heads, has_vres=has_vres),
        grid=(bsz, ngrp, nc),
        in_specs=in_specs, out_specs=out_specs, out_shape=out_shape,
        scratch_shapes=scratch,
        compiler_params=pltpu.CompilerParams(dimension_semantics=("parallel", "parallel", "arbitrary"),
                                             vmem_limit_bytes=VMEM_LIMIT),
        name="rwkv7_vres" if has_vres else "rwkv7",
    )(*args)
    return (out, v_first) if has_vres else out


def _ssd_kernel(z_ref, xbc_ref, dt_ref, cw_ref, cb_ref, hp_ref, cp_ref, e_ref, y_ref, prev_ref, st_ref, *, pad):
    c = pl.program_id(1)
    ck = CHUNK
    gp = B_HEADS_PER_GROUP * B_HEAD

    @pl.when(c == 0)
    def _():
        prev_ref[...] = jnp.zeros(prev_ref.shape, F32)
        st_ref[...] = jnp.zeros(st_ref.shape, F32)

    row = lax.broadcasted_iota(jnp.int32, (ck, 1), 0)
    x_raw = xbc_ref[0].astype(F32)
    prev = prev_ref[...]
    cw = cw_ref[...]
    conv = cb_ref[...] + x_raw * cw[B_CONV - 1:B_CONV]
    for s in range(1, B_CONV):
        conv = conv + _shift_rows(x_raw, prev, s, row) * cw[B_CONV - 1 - s:B_CONV - s]
    prev_ref[...] = x_raw
    xa = _silu(conv)
    valid = (c * ck + row) >= pad
    xs = jnp.where(valid, xa[:, :B_WIDTH], 0.0)
    bm = xa[:, B_WIDTH:B_WIDTH + B_GROUPS * B_STATE]
    cm = xa[:, B_WIDTH + B_GROUPS * B_STATE:]

    hp = hp_ref[...]
    dt = _softplus(dt_ref[0].astype(F32) + hp[0:1])
    a = dt * (-jnp.exp(hp[1:2]))

    ri = lax.broadcasted_iota(jnp.int32, (ck, ck), 0)
    ci = lax.broadcasted_iota(jnp.int32, (ck, ck), 1)
    causal = ri >= ci
    tril = jnp.where(causal, 1.0, 0.0).astype(BF16)
    triu = jnp.where(ri <= ci, 1.0, 0.0).astype(BF16)
    a_cs = _mm_exact_lhs(tril, a)
    a_cs_t = _mm_exact_rhs(a, triu, TN)
    e = e_ref[...]
    dt_e = _mm_exact_rhs(dt, e)
    acs_e = _mm_exact_rhs(a_cs, e)
    atot_e = acs_e[ck - 1:ck, :]
    xdt = xs * dt_e
    xw = xdt * jnp.exp(atot_e - acs_e)
    e_in = jnp.exp(acs_e)
    e_tot = jnp.exp(atot_e)

    cp = cp_ref[...]
    d_skip, norm_g = cp[0:1], cp[1:2]
    z = z_ref[0].astype(F32)
    outs = []
    for g in range(B_GROUPS):
        gs = slice(g * gp, (g + 1) * gp)
        bg = bm[:, g * B_STATE:(g + 1) * B_STATE]
        cg = cm[:, g * B_STATE:(g + 1) * B_STATE]
        cb = _mm(cg, bg, NT)
        st = st_ref[:, gs]
        y_off = _mm(cg, st) * e_in[:, gs]
        y_diag = []
        for r in range(B_HEADS_PER_GROUP):
            h = g * B_HEADS_PER_GROUP + r
            hs = slice(h * B_HEAD, (h + 1) * B_HEAD)
            seg = acs_e[:, hs] - a_cs_t[h:h + 1, :]
            lmat = jnp.exp(jnp.where(causal, seg, -1e30))
            y_diag.append(_mm(cb * lmat, xdt[:, hs]))
        st_ref[:, gs] = st * e_tot[:, gs] + _mm(bg, xw[:, gs], TN)
        y = jnp.concatenate(y_diag, axis=-1) + y_off
        y = (y + xs[:, gs] * d_skip[:, gs]) * _silu(z[:, gs])
        y = y * lax.rsqrt(jnp.mean(y * y, axis=-1, keepdims=True) + EPS)
        outs.append(y * norm_g[:, gs])
    y_ref[0] = jnp.concatenate(outs, axis=-1).astype(y_ref.dtype)


def _ssd_mix(proj, conv_w, conv_b, head_params, chan_params, expand, *, pad):
    bsz, lp, _ = proj.shape
    nc = lp // CHUNK

    def col(off, width):
        return pl.BlockSpec((1, CHUNK, width), lambda b, c, o=off // width: (b, c, o))

    def full(shape):
        return pl.BlockSpec(shape, lambda b, c: (0,) * len(shape))

    return pl.pallas_call(
        functools.partial(_ssd_kernel, pad=pad),
        grid=(bsz, nc),
        in_specs=[col(COL_Z, B_WIDTH), col(COL_XBC, B_CONV_DIM), col(COL_DT, LANES),
                  full((8, B_CONV_DIM)), full((1, B_CONV_DIM)), full((8, LANES)), full((8, B_WIDTH)),
                  full((LANES, B_WIDTH))],
        out_specs=pl.BlockSpec((1, CHUNK, B_WIDTH), lambda b, c: (b, c, 0)),
        out_shape=jax.ShapeDtypeStruct((bsz, lp, B_WIDTH), BF16),
        scratch_shapes=[pltpu.VMEM((CHUNK, B_CONV_DIM), F32), pltpu.VMEM((B_STATE, B_WIDTH), F32)],
        compiler_params=pltpu.CompilerParams(dimension_semantics=("parallel", "arbitrary"),
                                             vmem_limit_bytes=VMEM_LIMIT),
        name="ssd",
    )(proj, proj, proj, conv_w, conv_b, head_params, chan_params, expand)


def _retention_kernel(q_ref, k_ref, v_ref, g_ref, cos_ref, sin_ref, dec_ref, y_ref, st_ref):
    c = pl.program_id(1)
    ck = CHUNK

    @pl.when(c == 0)
    def _():
        st_ref[...] = jnp.zeros(st_ref.shape, F32)

    cos_t = cos_ref[...]
    sin_t = sin_ref[...]
    even = (lax.broadcasted_iota(jnp.int32, (1, C_QK), 1) % 2) == 0

    def rot(t):
        partner = jnp.where(even, pltpu.roll(t, C_QK - 1, 1), pltpu.roll(t, 1, 1))
        return t * cos_t + partner * sin_t

    q = rot(q_ref[0].astype(F32))
    k = rot(k_ref[0].astype(F32)) * (C_QK_HEAD ** -0.5)
    v = v_ref[0]
    gate = _silu(g_ref[0].astype(F32))

    hr = range(C_HEADS)
    qh = [q[:, h * C_QK_HEAD:(h + 1) * C_QK_HEAD] for h in hr]
    kh = [k[:, h * C_QK_HEAD:(h + 1) * C_QK_HEAD] for h in hr]
    vh = [v[:, h * C_V_HEAD:(h + 1) * C_V_HEAD].astype(BF16) for h in hr]
    s0 = [st_ref[h] for h in hr]
    scores = [_mm(qh[h], kh[h], NT) * dec_ref[3 * h] for h in hr]
    y = [_dg(scores[h].astype(BF16), vh[h], NN) + _mm(qh[h] * dec_ref[3 * h + 1], s0[h]) for h in hr]
    for h in hr:
        chunk_decay = (1.0 - 2.0 ** (-5.0 - h)) ** ck
        st_ref[h] = s0[h] * chunk_decay + _dg((kh[h] * dec_ref[3 * h + 2]).astype(BF16), vh[h], TN)
    yc = [t - jnp.mean(t, axis=-1, keepdims=True) for t in y]
    outs = [t * lax.rsqrt(jnp.mean(t * t, axis=-1, keepdims=True) + EPS) for t in yc]
    y_ref[0] = (gate * jnp.concatenate(outs, axis=-1)).astype(y_ref.dtype)


def _retention_decay_tables():
    idx = np.arange(CHUNK, dtype=np.float64)
    rel = idx[:, None] - idx[None, :]
    tabs = []
    for h in range(C_HEADS):
        log_gamma = math.log(1.0 - 2.0 ** (-5.0 - h))
        tabs.append(np.where(rel >= 0, np.exp(np.maximum(rel, 0.0) * log_gamma), 0.0))
        tabs.append(np.broadcast_to(np.exp((idx + 1.0) * log_gamma)[:, None], (CHUNK, C_QK_HEAD)))
        tabs.append(np.broadcast_to(np.exp((CHUNK - 1.0 - idx) * log_gamma)[:, None], (CHUNK, C_QK_HEAD)))
    return jnp.asarray(np.stack(tabs), F32)


def _retention_mix(proj, cos_t, sin_t):
    bsz, lp, _ = proj.shape
    nc = lp // CHUNK

    def col(off, width):
        return pl.BlockSpec((1, CHUNK, width), lambda b, c, o=off // width: (b, c, o))

    tab = pl.BlockSpec((CHUNK, C_QK), lambda b, c: (c, 0))
    return pl.pallas_call(
        _retention_kernel,
        grid=(bsz, nc),
        in_specs=[col(COL_RET_Q, C_QK), col(COL_RET_K, C_QK), col(COL_RET_V, C_WIDTH),
                  col(COL_RET_G, C_WIDTH), tab, tab,
                  pl.BlockSpec((3 * C_HEADS, CHUNK, CHUNK), lambda b, c: (0, 0, 0))],
        out_specs=pl.BlockSpec((1, CHUNK, C_WIDTH), lambda b, c: (b, c, 0)),
        out_shape=jax.ShapeDtypeStruct((bsz, lp, C_WIDTH), BF16),
        scratch_shapes=[pltpu.VMEM((C_HEADS, C_QK_HEAD, C_V_HEAD), F32)],
        compiler_params=pltpu.CompilerParams(dimension_semantics=("parallel", "arbitrary"),
                                             vmem_limit_bytes=VMEM_LIMIT),
        name="retention",
    )(proj, proj, proj, proj, cos_t, sin_t, _retention_decay_tables())


def _merge_kernel(ya_ref, yb_ref, yc_ref, g0_ref, g1_ref, g2_ref, gb_ref, h_ref,
                  wa_ref, wb_ref, wc_ref, wo_ref, o_ref):
    gb = gb_ref[...]
    d = D_MODEL
    merged = (_sigmoid(g0_ref[...].astype(F32) + gb[:, :d]) * _dg(ya_ref[...], wa_ref[...], NN)
              + _sigmoid(g1_ref[...].astype(F32) + gb[:, d:2 * d]) * _dg(yb_ref[...], wb_ref[...], NN)
              + _sigmoid(g2_ref[...].astype(F32) + gb[:, 2 * d:]) * _dg(yc_ref[...], wc_ref[...], NN))
    o_ref[...] = h_ref[...] + _dg(merged.astype(BF16), wo_ref[...], NN)


def _merge(ya, yb, yc, proj2d, gate_bias, h2d, wba, wbb, wbc, wo):
    m, d = h2d.shape
    tm = _largest_divisor(m, 352)

    def rows(width):
        return pl.BlockSpec((tm, width), lambda i: (i, 0))

    def gcol(j):
        return pl.BlockSpec((tm, d), lambda i, o=COL_GATE // d + j: (i, o))

    def full(shape):
        return pl.BlockSpec(shape, lambda i: (0, 0))

    return pl.pallas_call(
        _merge_kernel,
        grid=(m // tm,),
        in_specs=[rows(A_WIDTH), rows(B_WIDTH), rows(C_WIDTH), gcol(0), gcol(1), gcol(2),
                  full((1, GATE_IN)), rows(d),
                  full((A_WIDTH, d)), full((B_WIDTH, d)), full((C_WIDTH, d)), full((d, d))],
        out_specs=rows(d),
        out_shape=jax.ShapeDtypeStruct((m, d), F32),
        compiler_params=pltpu.CompilerParams(dimension_semantics=("parallel",),
                                             vmem_limit_bytes=VMEM_LIMIT),
        name="merge",
    )(ya, yb, yc, proj2d, proj2d, proj2d, gate_bias, h2d, wba, wbb, wbc, wo)


def _ffn_act_kernel(ug_ref, uv_ref, wg_ref, wv_ref, bg_ref, bv_ref, o_ref, pg_ref, pv_ref, *, tr):
    @pl.when(pl.program_id(2) == 0)
    def _():
        pg_ref[...] = jnp.zeros(pg_ref.shape, F32)
        pv_ref[...] = jnp.zeros(pv_ref.shape, F32)

    row = lax.broadcasted_iota(jnp.int32, (tr, 1), 0)

    def conv(x_ref, prev_ref, w_ref, b_ref):
        x = x_ref[0].astype(F32)
        prev = prev_ref[...]
        w = w_ref[...]
        out = b_ref[...] + x * w[FFN_CONV - 1:FFN_CONV]
        for s in range(1, FFN_CONV):
            out = out + _shift_rows(x, prev, s, row) * w[FFN_CONV - 1 - s:FFN_CONV - s]
        prev_ref[...] = x
        return out

    gate = conv(ug_ref, pg_ref, wg_ref, bg_ref)
    val = conv(uv_ref, pv_ref, wv_ref, bv_ref)
    o_ref[0] = (_silu(gate) * val).astype(o_ref.dtype)


def _ffn_act(up, conv_w, conv_b):
    bsz, lp, _ = up.shape
    tc = FFN_HIDDEN // 2
    ncol = FFN_HIDDEN // tc
    tr = _largest_divisor(lp, 704)

    def blk(off):
        return pl.BlockSpec((1, tr, tc), lambda b, j, i, o=off: (b, i, j + o))

    def par(rows, off):
        return pl.BlockSpec((rows, tc), lambda b, j, i, o=off: (0, j + o))

    return pl.pallas_call(
        functools.partial(_ffn_act_kernel, tr=tr),
        grid=(bsz, ncol, lp // tr),
        in_specs=[blk(0), blk(ncol), par(8, 0), par(8, ncol), par(1, 0), par(1, ncol)],
        out_specs=pl.BlockSpec((1, tr, tc), lambda b, j, i: (b, i, j)),
        out_shape=jax.ShapeDtypeStruct((bsz, lp, FFN_HIDDEN), BF16),
        scratch_shapes=[pltpu.VMEM((tr, tc), F32), pltpu.VMEM((tr, tc), F32)],
        compiler_params=pltpu.CompilerParams(dimension_semantics=("parallel", "parallel", "arbitrary"),
                                             vmem_limit_bytes=VMEM_LIMIT),
        name="ffn_conv_gate",
    )(up, up, conv_w, conv_w, conv_b, conv_b)


def _matmul_residual_kernel(a_ref, w_ref, h_ref, o_ref):
    o_ref[...] = h_ref[...] + _dg(a_ref[...], w_ref[...], NN)


def _matmul_residual(a, w, h2d):
    m, d = h2d.shape
    kdim = a.shape[1]
    tm = _largest_divisor(m, 1056)
    return pl.pallas_call(
        _matmul_residual_kernel,
        grid=(m // tm,),
        in_specs=[pl.BlockSpec((tm, kdim), lambda i: (i, 0)), pl.BlockSpec((kdim, d), lambda i: (0, 0)),
                  pl.BlockSpec((tm, d), lambda i: (i, 0))],
        out_specs=pl.BlockSpec((tm, d), lambda i: (i, 0)),
        out_shape=jax.ShapeDtypeStruct((m, d), F32),
        compiler_params=pltpu.CompilerParams(dimension_semantics=("parallel",),
                                             vmem_limit_bytes=VMEM_LIMIT),
        name="ffn_down",
    )(a, w, h2d)


def _final_norm_kernel(h_ref, g_ref, o_ref):
    x = h_ref[0]
    o_ref[0] = x * lax.rsqrt(jnp.mean(x * x, axis=-1, keepdims=True) + EPS) * g_ref[...]


def _final_norm(h3d, g, *, skip_chunks, seq):
    bsz, _, d = h3d.shape
    return pl.pallas_call(
        _final_norm_kernel,
        grid=(bsz, seq // CHUNK),
        in_specs=[pl.BlockSpec((1, CHUNK, d), lambda b, i: (b, i + skip_chunks, 0)),
                  pl.BlockSpec((1, d), lambda b, i: (0, 0))],
        out_specs=pl.BlockSpec((1, CHUNK, d), lambda b, i: (b, i, 0)),
        out_shape=jax.ShapeDtypeStruct((bsz, seq, d), F32),
        compiler_params=pltpu.CompilerParams(dimension_semantics=("parallel", "parallel")),
        name="final_norm",
    )(h3d, g.reshape(1, d))


def _pad_cols(t, width):
    return jnp.pad(t, [(0, 0)] * (t.ndim - 1) + [(0, width - t.shape[-1])])


def _pad_rows(t, rows):
    return jnp.pad(t, [(0, rows - t.shape[0])] + [(0, 0)] * (t.ndim - 1))


def _pack_in_proj(w, w_vres):
    a, b, c, g = (w[:, :A_IN], w[:, A_IN:A_IN + B_IN], w[:, A_IN + B_IN:A_IN + B_IN + C_IN],
                  w[:, A_IN + B_IN + C_IN:])
    aw = 3 * A_WIDTH
    parts = [b[:, B_WIDTH:B_WIDTH + B_CONV_DIM], g, b[:, :B_WIDTH],
             c[:, 2 * C_QK:2 * C_QK + C_WIDTH], c[:, 2 * C_QK + C_WIDTH:], c[:, :C_QK], c[:, C_QK:2 * C_QK],
             a[:, :aw], a[:, aw:aw + LANES], a[:, aw + LANES:],
             _pad_cols(w_vres, LANES), _pad_cols(b[:, B_WIDTH + B_CONV_DIM:], LANES)]
    out = jnp.concatenate(parts, axis=1)
    assert out.shape[1] == N_PROJ
    return out.astype(BF16)


def kernel(x, meta, norm_mix, norm_ffn, norm_final, w_in, w_in_vres, gate_bias, rwkv_mu, rwkv_mu_vres, rwkv_w0, rwkv_w2, rwkv_a0, rwkv_a2, rwkv_v0, rwkv_v2, rwkv_g2, rwkv_k_k, rwkv_k_a, rwkv_r_k, rwkv_ln_g, rwkv_ln_b, ssm_conv_w, ssm_conv_b, ssm_dt_bias, ssm_a_log, ssm_d, ssm_norm_g, w_branch, w_out, ffn_w_up, ffn_conv_w, ffn_conv_b, ffn_w_down):
    bsz, seq, d = x.shape
    depth = w_in.shape[0]
    length = N_META + seq
    pad = (-length) % CHUNK
    lp = pad + length
    assert (pad + N_META) % CHUNK == 0 and seq % CHUNK == 0
    m = bsz * lp

    h = jnp.concatenate([jnp.zeros((bsz, pad, d), F32),
                         jnp.broadcast_to(meta.astype(F32), (bsz, N_META, d)), x.astype(F32)], axis=1)
    h = h.reshape(m, d)

    pos = jnp.arange(length, dtype=F32)
    inv_freq = 1.0 / (ROPE_BASE ** jnp.linspace(0.0, 1.0, C_QK_HEAD // 2, dtype=F32))
    ang = pos[:, None] * inv_freq[None, :]
    cos_t = jnp.tile(jnp.repeat(jnp.cos(ang), 2, axis=1), (1, C_HEADS))
    sin_t = jnp.tile((jnp.repeat(jnp.sin(ang), 2, axis=1)
                      * jnp.tile(jnp.array([-1.0, 1.0], F32), C_QK_HEAD // 2)), (1, C_HEADS))
    cos_t = jnp.pad(cos_t, ((pad, 0), (0, 0)))
    sin_t = jnp.pad(sin_t, ((pad, 0), (0, 0)))

    expand = (jnp.arange(LANES)[:, None] == (jnp.arange(B_WIDTH)[None, :] // B_HEAD)).astype(BF16)

    heads = RWKV_HEADS_PER_STEP
    gw = heads * A_HEAD
    ngrp = A_WIDTH // gw
    aw = 3 * A_WIDTH
    v_first = None
    for l in range(depth):
        has_vres = l > 0
        w_vres = w_in_vres[l - 1] if has_vres else jnp.zeros((d, A_VRES_LORA), F32)
        proj = _norm_matmul(h, norm_mix[l], _pack_in_proj(w_in[l], w_vres), lp=lp, pad=pad, tn=512,
                            out_dtype=BF16)
        proj3 = proj.reshape(bsz, lp, N_PROJ)

        mu = rwkv_mu[l]
        zeros_w = jnp.zeros((A_WIDTH,), F32)
        pvec = jnp.stack([mu[:A_WIDTH], mu[A_WIDTH:2 * A_WIDTH], mu[2 * A_WIDTH:aw],
                          rwkv_w0[l], rwkv_a0[l], rwkv_k_k[l], rwkv_k_a[l], rwkv_r_k[l].reshape(-1),
                          rwkv_ln_g[l], rwkv_ln_b[l], rwkv_v0[l - 1] if has_vres else zeros_w]
                         + [zeros_w] * 5)
        mu_vr = _pad_cols(rwkv_mu_vres[l - 1], LANES) if has_vres else jnp.zeros((LANES,), F32)
        lmu = _pad_rows(jnp.stack([mu[aw:aw + LANES], mu[aw + LANES:], mu_vr]), 8)
        w2g = rwkv_w2[l].reshape(A_DECAY_LORA, ngrp, gw).transpose(1, 0, 2)
        a2g = rwkv_a2[l].reshape(A_ICLR_LORA, ngrp, gw).transpose(1, 0, 2)
        wwa = jnp.concatenate([jnp.concatenate([w2g, jnp.zeros_like(w2g)], axis=2),
                               jnp.concatenate([jnp.zeros_like(a2g), a2g], axis=2)], axis=1).astype(BF16)
        v2 = _pad_rows(rwkv_v2[l - 1], LANES).astype(BF16) if has_vres else None
        y_a, v_first = _rwkv_mix(proj3, v_first, pvec, lmu, wwa, rwkv_g2[l].astype(BF16), v2,
                                 has_vres=has_vres)

        head_params = _pad_rows(jnp.stack([_pad_cols(ssm_dt_bias[l], LANES), _pad_cols(ssm_a_log[l], LANES)]), 8)
        chan_params = _pad_rows(jnp.stack([jnp.repeat(ssm_d[l], B_HEAD), ssm_norm_g[l]]), 8)
        y_b = _ssd_mix(proj3, _pad_rows(ssm_conv_w[l], 8), ssm_conv_b[l].reshape(1, -1), head_params,
                       chan_params, expand, pad=pad)

        y_c = _retention_mix(proj3, cos_t, sin_t)

        wb = w_branch[l].astype(BF16)
        h = _merge(y_a.reshape(m, A_WIDTH), y_b.reshape(m, B_WIDTH), y_c.reshape(m, C_WIDTH), proj,
                   gate_bias[l].reshape(1, -1), h, wb[:A_WIDTH], wb[A_WIDTH:A_WIDTH + B_WIDTH],
                   wb[A_WIDTH + B_WIDTH:], w_out[l].astype(BF16))

        up = _norm_matmul(h, norm_ffn[l], ffn_w_up[l].astype(BF16), lp=lp, pad=pad, tn=512, out_dtype=BF16)
        act = _ffn_act(up.reshape(bsz, lp, 2 * FFN_HIDDEN), _pad_rows(ffn_conv_w[l], 8),
                       ffn_conv_b[l].reshape(1, -1))
        h = _matmul_residual(act.reshape(m, FFN_HIDDEN), ffn_w_down[l].astype(BF16), h)

    return _final_norm(h.reshape(bsz, lp, d), norm_final, skip_chunks=(pad + N_META) // CHUNK, seq=seq)
```

```python
import functools
import math

import numpy as np
import jax
import jax.numpy as jnp
from jax import lax
from jax.experimental import pallas as pl
from jax.experimental.pallas import tpu as pltpu

F32 = jnp.float32
BF16 = jnp.bfloat16

D_MODEL = 1024
CHUNK = 64
N_META = 16
EPS = 1e-6

A_HEAD = 64
A_WIDTH = D_MODEL
A_HEADS = A_WIDTH // A_HEAD
A_DECAY_LORA = 64
A_ICLR_LORA = 64
A_VRES_LORA = 32
A_GATE_LORA = 128
A_LN_EPS = 64e-5
A_IN = 3 * A_WIDTH + A_DECAY_LORA + A_ICLR_LORA + A_GATE_LORA

B_WIDTH = 2 * D_MODEL
B_HEAD = 64
B_HEADS = B_WIDTH // B_HEAD
B_GROUPS = 4
B_HEADS_PER_GROUP = B_HEADS // B_GROUPS
B_STATE = 128
B_CONV = 4
B_CONV_DIM = B_WIDTH + 2 * B_GROUPS * B_STATE
B_IN = B_WIDTH + B_CONV_DIM + B_HEADS

C_HEADS = 8
C_QK_HEAD = 64
C_V_HEAD = 128
C_QK = C_HEADS * C_QK_HEAD
C_WIDTH = C_HEADS * C_V_HEAD
C_IN = 2 * C_QK + 2 * C_WIDTH
ROPE_BASE = 10000.0

GATE_IN = 3 * D_MODEL
FFN_HIDDEN = 2816
FFN_CONV = 3

LANES = 128
SUBLANES = 8

COL_XBC = 0
COL_GATE = 3072
COL_Z = 6144
COL_RET_V = 8192
COL_RET_G = 9216
COL_RET_Q = 10240
COL_RET_K = 10752
COL_A_R = 11264
COL_A_K = 12288
COL_A_V = 13312
COL_A_WA = 14336
COL_A_GL = 14464
COL_A_VR = 14592
COL_DT = 14720
N_PROJ = 14848

RWKV_HEADS_PER_STEP = 16
CHUNKS_PER_STEP = 11
VMEM_LIMIT = 48 * 1024 * 1024
NORM_MATMUL_ROWS = 2112

NN = (((1,), (0,)), ((), ()))
NT = (((1,), (1,)), ((), ()))
TN = (((0,), (0,)), ((), ()))


def _dg(a, b, dims):
    return lax.dot_general(a, b, dims, preferred_element_type=F32)


def _split3(x):
    hi = x.astype(BF16)
    r = x - hi.astype(F32)
    mid = r.astype(BF16)
    lo = (r - mid.astype(F32)).astype(BF16)
    return hi, mid, lo


def _mm(a, b, dims=NN):
    return _dg(a.astype(BF16), b.astype(BF16), dims)


def _mm_exact_lhs(a_bf16, b, dims=NN):
    h, m, l = _split3(b)
    return _dg(a_bf16, h, dims) + (_dg(a_bf16, m, dims) + _dg(a_bf16, l, dims))


def _mm_exact_rhs(a, b_bf16, dims=NN):
    h, m, l = _split3(a)
    return _dg(h, b_bf16, dims) + (_dg(m, b_bf16, dims) + _dg(l, b_bf16, dims))


def _sigmoid(x):
    return 1.0 / (1.0 + jnp.exp(-x))


def _silu(x):
    return x * _sigmoid(x)


def _softplus(x):
    return jnp.maximum(x, 0.0) + jnp.log(1.0 + jnp.exp(-jnp.abs(x)))


def _shift_rows(x, tail, s):
    rolled = pltpu.roll(x, s, 0)
    row = lax.broadcasted_iota(jnp.int32, (SUBLANES, 1), 0)
    top = jnp.where(row < s, pltpu.roll(tail, s, 0), rolled[:SUBLANES])
    return jnp.concatenate([top, rolled[SUBLANES:]], axis=0)


def _rows_per_step(lp):
    return CHUNK * _largest_divisor(lp // CHUNK, CHUNKS_PER_STEP, 1)


def _largest_divisor(n, cap, mult=8):
    best = None
    for d in range(mult, cap + 1, mult):
        if n % d == 0:
            best = d
    assert best is not None
    return best


def _norm_matmul_kernel(h_ref, g_ref, w_ref, o_ref, u_ref, *, tm, lp, pad):
    i = pl.program_id(0)

    @pl.when(pl.program_id(1) == 0)
    def _():
        x = h_ref[...]
        y = x * lax.rsqrt(jnp.mean(x * x, axis=-1, keepdims=True) + EPS) * g_ref[...]
        pos = (i % (lp // tm)) * tm + lax.broadcasted_iota(jnp.int32, (tm, 1), 0)
        u_ref[...] = jnp.where(pos >= pad, y, 0.0).astype(BF16)

    o_ref[...] = jnp.dot(u_ref[...], w_ref[...], preferred_element_type=F32).astype(o_ref.dtype)


def _norm_matmul(h2d, g, w, *, lp, pad, tn, out_dtype):
    m, d = h2d.shape
    n = w.shape[1]
    tm = _largest_divisor(lp, NORM_MATMUL_ROWS)
    assert m % tm == 0 and n % tn == 0
    return pl.pallas_call(
        functools.partial(_norm_matmul_kernel, tm=tm, lp=lp, pad=pad),
        grid=(m // tm, n // tn),
        in_specs=[pl.BlockSpec((tm, d), lambda i, j: (i, 0)),
                  pl.BlockSpec((1, d), lambda i, j: (0, 0)),
                  pl.BlockSpec((d, tn), lambda i, j: (0, j))],
        out_specs=pl.BlockSpec((tm, tn), lambda i, j: (i, j)),
        out_shape=jax.ShapeDtypeStruct((m, n), out_dtype),
        scratch_shapes=[pltpu.VMEM((tm, d), BF16)],
        compiler_params=pltpu.CompilerParams(dimension_semantics=("parallel", "arbitrary"),
                                             vmem_limit_bytes=VMEM_LIMIT),
        name="norm_matmul",
    )(h2d, g.reshape(1, d), w)


def _unit_lower_inverse(ns, eye, same16, same32):
    ps = [jnp.where(same16, n, 0.0) for n in ns]
    ts = [eye + p for p in ps]
    for _ in range(3):
        ps = [_mm(p, p) for p in ps]
        ts = [t + _mm(t, p) for t, p in zip(ts, ps)]
    offs = [jnp.where(same32, n, 0.0) - jnp.where(same16, n, 0.0) for n in ns]
    tmp = [_mm(t, o) for t, o in zip(ts, offs)]
    ts = [t + _mm(x, t) for t, x in zip(ts, tmp)]
    offs = [jnp.where(same32, 0.0, n) for n in ns]
    tmp = [_mm(t, o) for t, o in zip(ts, offs)]
    return [t + _mm(x, t) for t, x in zip(ts, tmp)]


def _rwkv_kernel(*refs, heads, has_vres):
    it = iter(refs)
    pr_ref, pk_ref, pv_ref, pwa_ref, pgl_ref = (next(it) for _ in range(5))
    if has_vres:
        pvr_ref, vf_ref = next(it), next(it)
    pvec_ref, lmu_ref, wwa_ref, g2_ref = (next(it) for _ in range(4))
    if has_vres:
        v2_ref = next(it)
    y_ref = next(it)
    if not has_vres:
        vfo_ref = next(it)
    prev_r, prev_k, prev_v, prev_wa, prev_gl = (next(it) for _ in range(5))
    if has_vres:
        prev_vr = next(it)
    st_ref = next(it)

    ck = CHUNK
    gw = heads * A_HEAD
    tails = (prev_r, prev_k, prev_v, prev_wa, prev_gl) + ((prev_vr,) if has_vres else ())

    @pl.when(pl.program_id(2) == 0)
    def _():
        for ref in tails:
            ref[...] = jnp.zeros(ref.shape, F32)
        st_ref[...] = jnp.zeros(st_ref.shape, F32)

    pvec = pvec_ref[...]
    mu_r, mu_k, mu_v, w0, a0, k_k, k_a, r_k, ln_g, ln_b, v0 = (pvec[i:i + 1] for i in range(11))
    lmu = lmu_ref[...]
    lane = lax.broadcasted_iota(jnp.int32, (1, LANES), 1)
    ri = lax.broadcasted_iota(jnp.int32, (ck, ck), 0)
    ci = lax.broadcasted_iota(jnp.int32, (ck, ck), 1)
    strict = ri > ci
    same16 = (ri // 16) == (ci // 16)
    same32 = (ri // 32) == (ci // 32)
    eye = jnp.where(ri == ci, 1.0, 0.0).astype(F32)
    tril = jnp.where(ri >= ci, 1.0, 0.0).astype(BF16)
    incl = ri >= ci
    hsl = [slice(h * A_HEAD, (h + 1) * A_HEAD) for h in range(heads)]

    def chunk(i, carry):
        rows = pl.ds(pl.multiple_of(i * ck, ck), ck)

        def tshift(x_ref, tail_ref, mu):
            x = x_ref[0, rows, :].astype(F32)
            prev = _shift_rows(x, tail_ref[...], 1)
            tail_ref[...] = x[ck - SUBLANES:]
            return x + (prev - x) * mu

        r = tshift(pr_ref, prev_r, mu_r)
        k = tshift(pk_ref, prev_k, mu_k)
        v = tshift(pv_ref, prev_v, mu_v)
        wa = tshift(pwa_ref, prev_wa, lmu[0:1])
        gl = tshift(pgl_ref, prev_gl, lmu[1:2])

        wa_act = jnp.where(lane < A_DECAY_LORA, jnp.tanh(wa), wa)
        wa_out = _mm(wa_act, wwa_ref[0])
        w_log = -_softplus(-(w0 + wa_out[:, :gw])) - 0.5
        lw = -jnp.exp(w_log)
        a = _sigmoid(a0 + wa_out[:, gw:])
        g = _mm(_sigmoid(gl), g2_ref[...])
        if has_vres:
            vr = tshift(pvr_ref, prev_vr, lmu[2:3])
            v = v + (vf_ref[0, rows, :] - v) * _sigmoid(v0 + _mm(vr, v2_ref[...]))
        else:
            vfo_ref[0, rows, :] = v

        cum = _mm_exact_lhs(tril, lw)
        cum_last = cum[ck - 1:ck, :]
        e_neg = jnp.exp(-cum)
        e_end = jnp.exp(cum_last - cum)
        w_end = jnp.exp(cum_last)

        kk_raw = k * k_k
        kka = kk_raw * a
        kh = k * (1.0 + (a - 1.0) * k_a)
        rt_all = r * jnp.exp(cum)
        at_all = -(kk_raw * jnp.exp(cum - lw))
        khat_all = kh * e_neg
        bhat_all = kka * e_neg
        kt_all = kh * e_end
        bt_all = kka * e_end
        rk_all = r * kh * r_k

        def head_group(hids):
            hsl_ = [hsl[h] for h in hids]
            nrm = [lax.rsqrt(jnp.maximum(jnp.sum(kk_raw[:, hs] * kk_raw[:, hs], axis=-1, keepdims=True), 1e-24))
                   for hs in hsl_]
            vh = [v[:, hs] for hs in hsl_]
            s0 = [st_ref[h] for h in hids]
            at = [at_all[:, hs] * n for hs, n in zip(hsl_, nrm)]
            bhat = [bhat_all[:, hs] * n for hs, n in zip(hsl_, nrm)]
            bt = [bt_all[:, hs] * n for hs, n in zip(hsl_, nrm)]
            ar = [jnp.concatenate([x, rt_all[:, hs]], axis=0) for x, hs in zip(at, hsl_)]
            mk = [_mm(x, khat_all[:, hs], NT) for x, hs in zip(ar, hsl_)]
            mb = [_mm(x, y, NT) for x, y in zip(ar, bhat)]
            a_ak = [jnp.where(strict, x[:ck], 0.0) for x in mk]
            a_rk = [jnp.where(incl, x[ck:], 0.0) for x in mk]
            a_ab = [jnp.where(strict, x[:ck], 0.0) for x in mb]
            a_rb = [jnp.where(incl, x[ck:], 0.0) for x in mb]
            tinv = _unit_lower_inverse(a_ab, eye, same16, same32)
            ars = [_mm(x, s, NT) for x, s in zip(ar, s0)]
            rhs = [x[:ck] + _mm(m_, v_) for x, m_, v_ in zip(ars, a_ak, vh)]
            u = [_mm(t, x) for t, x in zip(tinv, rhs)]
            ys = [x[ck:] + _mm(m_, v_) + _mm(n_, u_) for x, m_, v_, n_, u_ in zip(ars, a_rk, vh, a_rb, u)]
            res = []
            for n_, h in enumerate(hids):
                hs = hsl[h]
                st_ref[h] = s0[n_] * w_end[:, hs] + _mm(vh[n_], kt_all[:, hs], TN) + _mm(u[n_], bt[n_], TN)
                yc = ys[n_] - jnp.mean(ys[n_], axis=-1, keepdims=True)
                yn = yc * lax.rsqrt(jnp.mean(yc * yc, axis=-1, keepdims=True) + A_LN_EPS)
                bonus = jnp.sum(rk_all[:, hs], axis=-1, keepdims=True) * vh[n_]
                res.append((yn * ln_g[:, hs] + ln_b[:, hs] + bonus) * g[:, hs])
            return res

        outs = head_group(list(range(heads)))
        y_ref[0, rows, :] = jnp.concatenate(outs, axis=-1).astype(y_ref.dtype)
        return carry

    lax.fori_loop(0, pr_ref.shape[1] // ck, chunk, 0)


def _rwkv_mix(proj, v_first, pvec, lmu, wwa, g2, v2, *, has_vres):
    bsz, lp, _ = proj.shape
    heads = RWKV_HEADS_PER_STEP
    gw = heads * A_HEAD
    ngrp = A_WIDTH // gw
    tr = _rows_per_step(lp)

    def col(off, width):
        return pl.BlockSpec((1, tr, width), lambda b, hg, s, o=off // width: (b, s, o))

    def colg(off):
        return pl.BlockSpec((1, tr, gw), lambda b, hg, s, o=off // gw: (b, s, o + hg))

    act_g = pl.BlockSpec((1, tr, gw), lambda b, hg, s: (b, s, hg))
    in_specs = [colg(COL_A_R), colg(COL_A_K), colg(COL_A_V), col(COL_A_WA, LANES), col(COL_A_GL, LANES)]
    args = [proj] * 5
    if has_vres:
        in_specs += [col(COL_A_VR, LANES), act_g]
        args += [proj, v_first]
    in_specs += [pl.BlockSpec((16, gw), lambda b, hg, s: (0, hg)),
                 pl.BlockSpec((8, LANES), lambda b, hg, s: (0, 0)),
                 pl.BlockSpec((1, LANES, 2 * gw), lambda b, hg, s: (hg, 0, 0)),
                 pl.BlockSpec((A_GATE_LORA, gw), lambda b, hg, s: (0, hg))]
    args += [pvec, lmu, wwa, g2]
    if has_vres:
        in_specs.append(pl.BlockSpec((LANES, gw), lambda b, hg, s: (0, hg)))
        args.append(v2)
    y_shape = jax.ShapeDtypeStruct((bsz, lp, A_WIDTH), BF16)
    if has_vres:
        out_shape, out_specs = y_shape, act_g
    else:
        out_shape = (y_shape, jax.ShapeDtypeStruct((bsz, lp, A_WIDTH), F32))
        out_specs = (act_g, act_g)
    scratch = ([pltpu.VMEM((SUBLANES, gw), F32)] * 3
               + [pltpu.VMEM((SUBLANES, LANES), F32)] * (3 if has_vres else 2))
    scratch.append(pltpu.VMEM((heads, A_HEAD, A_HEAD), F32))
    out = pl.pallas_call(
        functools.partial(_rwkv_kernel, heads=heads, has_vres=has_vres),
        grid=(bsz, ngrp, lp // tr),
        in_specs=in_specs, out_specs=out_specs, out_shape=out_shape,
        scratch_shapes=scratch,
        compiler_params=pltpu.CompilerParams(dimension_semantics=("parallel", "parallel", "arbitrary"),
                                             vmem_limit_bytes=VMEM_LIMIT),
        name="rwkv7_vres" if has_vres else "rwkv7",
    )(*args)
    return (out, v_first) if has_vres else out


def _ssd_kernel(z_ref, xbc_ref, dt_ref, cw_ref, cb_ref, hp_ref, cp_ref, e_ref, y_ref, tail_ref, st_ref, *, pad):
    ck = CHUNK
    gp = B_HEADS_PER_GROUP * B_HEAD
    nsub = xbc_ref.shape[1] // ck
    step = pl.program_id(1)

    @pl.when(step == 0)
    def _():
        tail_ref[...] = jnp.zeros(tail_ref.shape, F32)
        st_ref[...] = jnp.zeros(st_ref.shape, F32)

    row = lax.broadcasted_iota(jnp.int32, (ck, 1), 0)
    ri = lax.broadcasted_iota(jnp.int32, (ck, ck), 0)
    ci = lax.broadcasted_iota(jnp.int32, (ck, ck), 1)
    causal = ri >= ci
    tril = jnp.where(causal, 1.0, 0.0).astype(BF16)
    triu = jnp.where(ri <= ci, 1.0, 0.0).astype(BF16)
    cw = cw_ref[...]
    hp = hp_ref[...]
    cp = cp_ref[...]
    d_skip, norm_g = cp[0:1], cp[1:2]

    def chunk(i, carry):
        rows = pl.ds(pl.multiple_of(i * ck, ck), ck)
        x_raw = xbc_ref[0, rows, :].astype(F32)
        tail = tail_ref[...]
        conv = cb_ref[...] + x_raw * cw[B_CONV - 1:B_CONV]
        for s in range(1, B_CONV):
            conv = conv + _shift_rows(x_raw, tail, s) * cw[B_CONV - 1 - s:B_CONV - s]
        tail_ref[...] = x_raw[ck - SUBLANES:]
        xa = _silu(conv)
        valid = ((step * nsub + i) * ck + row) >= pad
        xs = jnp.where(valid, xa[:, :B_WIDTH], 0.0)
        bm = xa[:, B_WIDTH:B_WIDTH + B_GROUPS * B_STATE]
        cm = xa[:, B_WIDTH + B_GROUPS * B_STATE:]

        dt = _softplus(dt_ref[0, rows, :].astype(F32) + hp[0:1])
        a = dt * (-jnp.exp(hp[1:2]))
        a_cs = _mm_exact_lhs(tril, a)
        a_cs_t = _mm_exact_rhs(a, triu, TN)
        e = e_ref[...]
        dt_e = _mm_exact_rhs(dt, e)
        acs_e = _mm_exact_rhs(a_cs, e)
        atot_e = acs_e[ck - 1:ck, :]
        xdt = xs * dt_e
        xw = xdt * jnp.exp(atot_e - acs_e)
        e_in = jnp.exp(acs_e)
        e_tot = jnp.exp(atot_e)

        z = z_ref[0, rows, :].astype(F32)
        outs = []
        for g in range(B_GROUPS):
            gs = slice(g * gp, (g + 1) * gp)
            bg = bm[:, g * B_STATE:(g + 1) * B_STATE]
            cg = cm[:, g * B_STATE:(g + 1) * B_STATE]
            cb = _mm(cg, bg, NT)
            st = st_ref[:, gs]
            y_off = _mm(cg, st) * e_in[:, gs]
            y_diag = []
            for r in range(B_HEADS_PER_GROUP):
                h = g * B_HEADS_PER_GROUP + r
                hs = slice(h * B_HEAD, (h + 1) * B_HEAD)
                seg = acs_e[:, hs] - a_cs_t[h:h + 1, :]
                lmat = jnp.exp(jnp.where(causal, seg, -1e30))
                y_diag.append(_mm(cb * lmat, xdt[:, hs]))
            st_ref[:, gs] = st * e_tot[:, gs] + _mm(bg, xw[:, gs], TN)
            y = jnp.concatenate(y_diag, axis=-1) + y_off
            y = (y + xs[:, gs] * d_skip[:, gs]) * _silu(z[:, gs])
            y = y * lax.rsqrt(jnp.mean(y * y, axis=-1, keepdims=True) + EPS)
            outs.append(y * norm_g[:, gs])
        y_ref[0, rows, :] = jnp.concatenate(outs, axis=-1).astype(y_ref.dtype)
        return carry

    lax.fori_loop(0, nsub, chunk, 0)


def _ssd_mix(proj, conv_w, conv_b, head_params, chan_params, expand, *, pad):
    bsz, lp, _ = proj.shape
    tr = _rows_per_step(lp)

    def col(off, width):
        return pl.BlockSpec((1, tr, width), lambda b, s, o=off // width: (b, s, o))

    def full(shape):
        return pl.BlockSpec(shape, lambda b, s: (0,) * len(shape))

    return pl.pallas_call(
        functools.partial(_ssd_kernel, pad=pad),
        grid=(bsz, lp // tr),
        in_specs=[col(COL_Z, B_WIDTH), col(COL_XBC, B_CONV_DIM), col(COL_DT, LANES),
                  full((8, B_CONV_DIM)), full((1, B_CONV_DIM)), full((8, LANES)), full((8, B_WIDTH)),
                  full((LANES, B_WIDTH))],
        out_specs=pl.BlockSpec((1, tr, B_WIDTH), lambda b, s: (b, s, 0)),
        out_shape=jax.ShapeDtypeStruct((bsz, lp, B_WIDTH), BF16),
        scratch_shapes=[pltpu.VMEM((SUBLANES, B_CONV_DIM), F32), pltpu.VMEM((B_STATE, B_WIDTH), F32)],
        compiler_params=pltpu.CompilerParams(dimension_semantics=("parallel", "arbitrary"),
                                             vmem_limit_bytes=VMEM_LIMIT),
        name="ssd",
    )(proj, proj, proj, conv_w, conv_b, head_params, chan_params, expand)


def _retention_kernel(q_ref, k_ref, v_ref, g_ref, cos_ref, sin_ref, dec_ref, y_ref, st_ref):
    ck = CHUNK

    @pl.when(pl.program_id(1) == 0)
    def _():
        st_ref[...] = jnp.zeros(st_ref.shape, F32)

    even = (lax.broadcasted_iota(jnp.int32, (1, C_QK), 1) % 2) == 0
    hr = range(C_HEADS)

    def chunk(i, carry):
        rows = pl.ds(pl.multiple_of(i * ck, ck), ck)
        cos_t = cos_ref[rows, :]
        sin_t = sin_ref[rows, :]

        def rot(t):
            partner = jnp.where(even, pltpu.roll(t, C_QK - 1, 1), pltpu.roll(t, 1, 1))
            return t * cos_t + partner * sin_t

        q = rot(q_ref[0, rows, :].astype(F32))
        k = rot(k_ref[0, rows, :].astype(F32)) * (C_QK_HEAD ** -0.5)
        v = v_ref[0, rows, :]
        gate = _silu(g_ref[0, rows, :].astype(F32))

        qh = [q[:, h * C_QK_HEAD:(h + 1) * C_QK_HEAD] for h in hr]
        kh = [k[:, h * C_QK_HEAD:(h + 1) * C_QK_HEAD] for h in hr]
        vh = [v[:, h * C_V_HEAD:(h + 1) * C_V_HEAD].astype(BF16) for h in hr]
        s0 = [st_ref[h] for h in hr]
        scores = [_mm(qh[h], kh[h], NT) * dec_ref[3 * h] for h in hr]
        y = [_dg(scores[h].astype(BF16), vh[h], NN) + _mm(qh[h] * dec_ref[3 * h + 1], s0[h]) for h in hr]
        for h in hr:
            chunk_decay = (1.0 - 2.0 ** (-5.0 - h)) ** ck
            st_ref[h] = s0[h] * chunk_decay + _dg((kh[h] * dec_ref[3 * h + 2]).astype(BF16), vh[h], TN)
        yc = [t - jnp.mean(t, axis=-1, keepdims=True) for t in y]
        outs = [t * lax.rsqrt(jnp.mean(t * t, axis=-1, keepdims=True) + EPS) for t in yc]
        y_ref[0, rows, :] = (gate * jnp.concatenate(outs, axis=-1)).astype(y_ref.dtype)
        return carry

    lax.fori_loop(0, q_ref.shape[1] // ck, chunk, 0)


def _retention_decay_tables():
    idx = np.arange(CHUNK, dtype=np.float64)
    rel = idx[:, None] - idx[None, :]
    tabs = []
    for h in range(C_HEADS):
        log_gamma = math.log(1.0 - 2.0 ** (-5.0 - h))
        tabs.append(np.where(rel >= 0, np.exp(np.maximum(rel, 0.0) * log_gamma), 0.0))
        tabs.append(np.broadcast_to(np.exp((idx + 1.0) * log_gamma)[:, None], (CHUNK, C_QK_HEAD)))
        tabs.append(np.broadcast_to(np.exp((CHUNK - 1.0 - idx) * log_gamma)[:, None], (CHUNK, C_QK_HEAD)))
    return jnp.asarray(np.stack(tabs), F32)


def _retention_mix(proj, cos_t, sin_t):
    bsz, lp, _ = proj.shape
    tr = _rows_per_step(lp)

    def col(off, width):
        return pl.BlockSpec((1, tr, width), lambda b, s, o=off // width: (b, s, o))

    tab = pl.BlockSpec((tr, C_QK), lambda b, s: (s, 0))
    return pl.pallas_call(
        _retention_kernel,
        grid=(bsz, lp // tr),
        in_specs=[col(COL_RET_Q, C_QK), col(COL_RET_K, C_QK), col(COL_RET_V, C_WIDTH),
                  col(COL_RET_G, C_WIDTH), tab, tab,
                  pl.BlockSpec((3 * C_HEADS, CHUNK, CHUNK), lambda b, s: (0, 0, 0))],
        out_specs=pl.BlockSpec((1, tr, C_WIDTH), lambda b, s: (b, s, 0)),
        out_shape=jax.ShapeDtypeStruct((bsz, lp, C_WIDTH), BF16),
        scratch_shapes=[pltpu.VMEM((C_HEADS, C_QK_HEAD, C_V_HEAD), F32)],
        compiler_params=pltpu.CompilerParams(dimension_semantics=("parallel", "arbitrary"),
                                             vmem_limit_bytes=VMEM_LIMIT),
        name="retention",
    )(proj, proj, proj, proj, cos_t, sin_t, _retention_decay_tables())


def _merge_kernel(ya_ref, yb_ref, yc_ref, g0_ref, g1_ref, g2_ref, gb_ref, h_ref,
                  wa_ref, wb_ref, wc_ref, wo_ref, o_ref):
    gb = gb_ref[...]
    d = D_MODEL
    merged = (_sigmoid(g0_ref[...].astype(F32) + gb[:, :d]) * _dg(ya_ref[...], wa_ref[...], NN)
              + _sigmoid(g1_ref[...].astype(F32) + gb[:, d:2 * d]) * _dg(yb_ref[...], wb_ref[...], NN)
              + _sigmoid(g2_ref[...].astype(F32) + gb[:, 2 * d:]) * _dg(yc_ref[...], wc_ref[...], NN))
    o_ref[...] = h_ref[...] + _dg(merged.astype(BF16), wo_ref[...], NN)


def _merge(ya, yb, yc, proj2d, gate_bias, h2d, wba, wbb, wbc, wo):
    m, d = h2d.shape
    tm = _largest_divisor(m, 352)

    def rows(width):
        return pl.BlockSpec((tm, width), lambda i: (i, 0))

    def gcol(j):
        return pl.BlockSpec((tm, d), lambda i, o=COL_GATE // d + j: (i, o))

    def full(shape):
        return pl.BlockSpec(shape, lambda i: (0, 0))

    return pl.pallas_call(
        _merge_kernel,
        grid=(m // tm,),
        in_specs=[rows(A_WIDTH), rows(B_WIDTH), rows(C_WIDTH), gcol(0), gcol(1), gcol(2),
                  full((1, GATE_IN)), rows(d),
                  full((A_WIDTH, d)), full((B_WIDTH, d)), full((C_WIDTH, d)), full((d, d))],
        out_specs=rows(d),
        out_shape=jax.ShapeDtypeStruct((m, d), F32),
        compiler_params=pltpu.CompilerParams(dimension_semantics=("parallel",),
                                             vmem_limit_bytes=VMEM_LIMIT),
        name="merge",
    )(ya, yb, yc, proj2d, proj2d, proj2d, gate_bias, h2d, wba, wbb, wbc, wo)


def _ffn_act_kernel(ug_ref, uv_ref, wg_ref, wv_ref, bg_ref, bv_ref, o_ref, tg_ref, tv_ref, *, tr):
    @pl.when(pl.program_id(2) == 0)
    def _():
        tg_ref[...] = jnp.zeros(tg_ref.shape, F32)
        tv_ref[...] = jnp.zeros(tv_ref.shape, F32)

    def conv(x_ref, tail_ref, w_ref, b_ref):
        x = x_ref[0].astype(F32)
        tail = tail_ref[...]
        w = w_ref[...]
        out = b_ref[...] + x * w[FFN_CONV - 1:FFN_CONV]
        for s in range(1, FFN_CONV):
            out = out + _shift_rows(x, tail, s) * w[FFN_CONV - 1 - s:FFN_CONV - s]
        tail_ref[...] = x[tr - SUBLANES:]
        return out

    gate = conv(ug_ref, tg_ref, wg_ref, bg_ref)
    val = conv(uv_ref, tv_ref, wv_ref, bv_ref)
    o_ref[0] = (_silu(gate) * val).astype(o_ref.dtype)


def _ffn_act(up, conv_w, conv_b):
    bsz, lp, _ = up.shape
    tc = FFN_HIDDEN // 2
    ncol = FFN_HIDDEN // tc
    tr = _largest_divisor(lp, 704)

    def blk(off):
        return pl.BlockSpec((1, tr, tc), lambda b, j, i, o=off: (b, i, j + o))

    def par(rows, off):
        return pl.BlockSpec((rows, tc), lambda b, j, i, o=off: (0, j + o))

    return pl.pallas_call(
        functools.partial(_ffn_act_kernel, tr=tr),
        grid=(bsz, ncol, lp // tr),
        in_specs=[blk(0), blk(ncol), par(8, 0), par(8, ncol), par(1, 0), par(1, ncol)],
        out_specs=pl.BlockSpec((1, tr, tc), lambda b, j, i: (b, i, j)),
        out_shape=jax.ShapeDtypeStruct((bsz, lp, FFN_HIDDEN), BF16),
        scratch_shapes=[pltpu.VMEM((SUBLANES, tc), F32), pltpu.VMEM((SUBLANES, tc), F32)],
        compiler_params=pltpu.CompilerParams(dimension_semantics=("parallel", "parallel", "arbitrary"),
                                             vmem_limit_bytes=VMEM_LIMIT),
        name="ffn_conv_gate",
    )(up, up, conv_w, conv_w, conv_b, conv_b)


def _matmul_residual_kernel(a_ref, w_ref, h_ref, o_ref):
    o_ref[...] = h_ref[...] + _dg(a_ref[...], w_ref[...], NN)


def _matmul_residual(a, w, h2d):
    m, d = h2d.shape
    kdim = a.shape[1]
    tm = _largest_divisor(m, 1056)
    return pl.pallas_call(
        _matmul_residual_kernel,
        grid=(m // tm,),
        in_specs=[pl.BlockSpec((tm, kdim), lambda i: (i, 0)), pl.BlockSpec((kdim, d), lambda i: (0, 0)),
                  pl.BlockSpec((tm, d), lambda i: (i, 0))],
        out_specs=pl.BlockSpec((tm, d), lambda i: (i, 0)),
        out_shape=jax.ShapeDtypeStruct((m, d), F32),
        compiler_params=pltpu.CompilerParams(dimension_semantics=("parallel",),
                                             vmem_limit_bytes=VMEM_LIMIT),
        name="ffn_down",
    )(a, w, h2d)


def _final_norm_kernel(h_ref, g_ref, o_ref, *, skip):
    x = h_ref[0, skip:, :]
    o_ref[0] = x * lax.rsqrt(jnp.mean(x * x, axis=-1, keepdims=True) + EPS) * g_ref[...]


def _final_norm(h3d, g, *, skip, seq):
    bsz, lp, d = h3d.shape
    assert lp == skip + seq
    return pl.pallas_call(
        functools.partial(_final_norm_kernel, skip=skip),
        grid=(bsz,),
        in_specs=[pl.BlockSpec((1, lp, d), lambda b: (b, 0, 0)), pl.BlockSpec((1, d), lambda b: (0, 0))],
        out_specs=pl.BlockSpec((1, seq, d), lambda b: (b, 0, 0)),
        out_shape=jax.ShapeDtypeStruct((bsz, seq, d), F32),
        compiler_params=pltpu.CompilerParams(dimension_semantics=("parallel",),
                                             vmem_limit_bytes=VMEM_LIMIT),
        name="final_norm",
    )(h3d, g.reshape(1, d))


def _pad_cols(t, width):
    return jnp.pad(t, [(0, 0)] * (t.ndim - 1) + [(0, width - t.shape[-1])])


def _pad_rows(t, rows):
    return jnp.pad(t, [(0, rows - t.shape[0])] + [(0, 0)] * (t.ndim - 1))


def _pack_in_proj(w, w_vres):
    a, b, c, g = (w[:, :A_IN], w[:, A_IN:A_IN + B_IN], w[:, A_IN + B_IN:A_IN + B_IN + C_IN],
                  w[:, A_IN + B_IN + C_IN:])
    aw = 3 * A_WIDTH
    parts = [b[:, B_WIDTH:B_WIDTH + B_CONV_DIM], g, b[:, :B_WIDTH],
             c[:, 2 * C_QK:2 * C_QK + C_WIDTH], c[:, 2 * C_QK + C_WIDTH:], c[:, :C_QK], c[:, C_QK:2 * C_QK],
             a[:, :aw], a[:, aw:aw + LANES], a[:, aw + LANES:],
             _pad_cols(w_vres, LANES), _pad_cols(b[:, B_WIDTH + B_CONV_DIM:], LANES)]
    out = jnp.concatenate(parts, axis=1)
    assert out.shape[1] == N_PROJ
    return out.astype(BF16)


def kernel(x, meta, norm_mix, norm_ffn, norm_final, w_in, w_in_vres, gate_bias, rwkv_mu, rwkv_mu_vres, rwkv_w0, rwkv_w2, rwkv_a0, rwkv_a2, rwkv_v0, rwkv_v2, rwkv_g2, rwkv_k_k, rwkv_k_a, rwkv_r_k, rwkv_ln_g, rwkv_ln_b, ssm_conv_w, ssm_conv_b, ssm_dt_bias, ssm_a_log, ssm_d, ssm_norm_g, w_branch, w_out, ffn_w_up, ffn_conv_w, ffn_conv_b, ffn_w_down):
    bsz, seq, d = x.shape
    depth = w_in.shape[0]
    length = N_META + seq
    pad = (-length) % CHUNK
    lp = pad + length
    assert (pad + N_META) % CHUNK == 0 and seq % CHUNK == 0
    m = bsz * lp

    h = jnp.concatenate([jnp.zeros((bsz, pad, d), F32),
                         jnp.broadcast_to(meta.astype(F32), (bsz, N_META, d)), x.astype(F32)], axis=1)
    h = h.reshape(m, d)

    pos = jnp.arange(length, dtype=F32)
    inv_freq = 1.0 / (ROPE_BASE ** jnp.linspace(0.0, 1.0, C_QK_HEAD // 2, dtype=F32))
    ang = pos[:, None] * inv_freq[None, :]
    cos_t = jnp.tile(jnp.repeat(jnp.cos(ang), 2, axis=1), (1, C_HEADS))
    sin_t = jnp.tile((jnp.repeat(jnp.sin(ang), 2, axis=1)
                      * jnp.tile(jnp.array([-1.0, 1.0], F32), C_QK_HEAD // 2)), (1, C_HEADS))
    cos_t = jnp.pad(cos_t, ((pad, 0), (0, 0)))
    sin_t = jnp.pad(sin_t, ((pad, 0), (0, 0)))

    expand = (jnp.arange(LANES)[:, None] == (jnp.arange(B_WIDTH)[None, :] // B_HEAD)).astype(BF16)

    heads = RWKV_HEADS_PER_STEP
    gw = heads * A_HEAD
    ngrp = A_WIDTH // gw
    aw = 3 * A_WIDTH
    v_first = None
    for l in range(depth):
        has_vres = l > 0
        w_vres = w_in_vres[l - 1] if has_vres else jnp.zeros((d, A_VRES_LORA), F32)
        proj = _norm_matmul(h, norm_mix[l], _pack_in_proj(w_in[l], w_vres), lp=lp, pad=pad, tn=512,
                            out_dtype=BF16)
        proj3 = proj.reshape(bsz, lp, N_PROJ)

        mu = rwkv_mu[l]
        zeros_w = jnp.zeros((A_WIDTH,), F32)
        pvec = jnp.stack([mu[:A_WIDTH], mu[A_WIDTH:2 * A_WIDTH], mu[2 * A_WIDTH:aw],
                          rwkv_w0[l], rwkv_a0[l], rwkv_k_k[l], rwkv_k_a[l], rwkv_r_k[l].reshape(-1),
                          rwkv_ln_g[l], rwkv_ln_b[l], rwkv_v0[l - 1] if has_vres else zeros_w]
                         + [zeros_w] * 5)
        mu_vr = _pad_cols(rwkv_mu_vres[l - 1], LANES) if has_vres else jnp.zeros((LANES,), F32)
        lmu = _pad_rows(jnp.stack([mu[aw:aw + LANES], mu[aw + LANES:], mu_vr]), 8)
        w2g = rwkv_w2[l].reshape(A_DECAY_LORA, ngrp, gw).transpose(1, 0, 2)
        a2g = rwkv_a2[l].reshape(A_ICLR_LORA, ngrp, gw).transpose(1, 0, 2)
        wwa = jnp.concatenate([jnp.concatenate([w2g, jnp.zeros_like(w2g)], axis=2),
                               jnp.concatenate([jnp.zeros_like(a2g), a2g], axis=2)], axis=1).astype(BF16)
        v2 = _pad_rows(rwkv_v2[l - 1], LANES).astype(BF16) if has_vres else None
        y_a, v_first = _rwkv_mix(proj3, v_first, pvec, lmu, wwa, rwkv_g2[l].astype(BF16), v2,
                                 has_vres=has_vres)

        head_params = _pad_rows(jnp.stack([_pad_cols(ssm_dt_bias[l], LANES), _pad_cols(ssm_a_log[l], LANES)]), 8)
        chan_params = _pad_rows(jnp.stack([jnp.repeat(ssm_d[l], B_HEAD), ssm_norm_g[l]]), 8)
        y_b = _ssd_mix(proj3, _pad_rows(ssm_conv_w[l], 8), ssm_conv_b[l].reshape(1, -1), head_params,
                       chan_params, expand, pad=pad)

        y_c = _retention_mix(proj3, cos_t, sin_t)

        wb = w_branch[l].astype(BF16)
        h = _merge(y_a.reshape(m, A_WIDTH), y_b.reshape(m, B_WIDTH), y_c.reshape(m, C_WIDTH), proj,
                   gate_bias[l].reshape(1, -1), h, wb[:A_WIDTH], wb[A_WIDTH:A_WIDTH + B_WIDTH],
                   wb[A_WIDTH + B_WIDTH:], w_out[l].astype(BF16))

        up = _norm_matmul(h, norm_ffn[l], ffn_w_up[l].astype(BF16), lp=lp, pad=pad, tn=512, out_dtype=BF16)
        act = _ffn_act(up.reshape(bsz, lp, 2 * FFN_HIDDEN), _pad_rows(ffn_conv_w[l], 8),
                       ffn_conv_b[l].reshape(1, -1))
        h = _matmul_residual(act.reshape(m, FFN_HIDDEN), ffn_w_down[l].astype(BF16), h)

    return _final_norm(h.reshape(bsz, lp, d), norm_final, skip=pad + N_META, seq=seq)
```

```python
import functools
import math

import numpy as np
import jax
import jax.numpy as jnp
from jax import lax
from jax.experimental import pallas as pl
from jax.experimental.pallas import tpu as pltpu

F32 = jnp.float32
BF16 = jnp.bfloat16

D_MODEL = 1024
CHUNK = 64
N_META = 16
EPS = 1e-6

A_HEAD = 64
A_WIDTH = D_MODEL
A_HEADS = A_WIDTH // A_HEAD
A_DECAY_LORA = 64
A_ICLR_LORA = 64
A_VRES_LORA = 32
A_GATE_LORA = 128
A_LN_EPS = 64e-5
A_IN = 3 * A_WIDTH + A_DECAY_LORA + A_ICLR_LORA + A_GATE_LORA

B_WIDTH = 2 * D_MODEL
B_HEAD = 64
B_HEADS = B_WIDTH // B_HEAD
B_GROUPS = 4
B_HEADS_PER_GROUP = B_HEADS // B_GROUPS
B_STATE = 128
B_CONV = 4
B_CONV_DIM = B_WIDTH + 2 * B_GROUPS * B_STATE
B_IN = B_WIDTH + B_CONV_DIM + B_HEADS

C_HEADS = 8
C_QK_HEAD = 64
C_V_HEAD = 128
C_QK = C_HEADS * C_QK_HEAD
C_WIDTH = C_HEADS * C_V_HEAD
C_IN = 2 * C_QK + 2 * C_WIDTH
ROPE_BASE = 10000.0

GATE_IN = 3 * D_MODEL
FFN_HIDDEN = 2816
FFN_CONV = 3
FFN_ROW_BLOCKS = 12
FFN_TILE = 256

LANES = 128
SUBLANES = 8

COL_XBC = 0
COL_GATE = 3072
COL_Z = 6144
COL_RET_V = 8192
COL_RET_G = 9216
COL_RET_Q = 10240
COL_RET_K = 10752
COL_A_R = 11264
COL_A_K = 12288
COL_A_V = 13312
COL_A_WA = 14336
COL_A_GL = 14464
COL_A_VR = 14592
COL_DT = 14720
N_PROJ = 14848

RWKV_HEADS_PER_STEP = 16
CHUNKS_PER_STEP = 11
VMEM_LIMIT = 48 * 1024 * 1024
NORM_MATMUL_ROWS = 2112

NN = (((1,), (0,)), ((), ()))
NT = (((1,), (1,)), ((), ()))
TN = (((0,), (0,)), ((), ()))


def _dg(a, b, dims):
    return lax.dot_general(a, b, dims, preferred_element_type=F32)


def _split3(x):
    hi = x.astype(BF16)
    r = x - hi.astype(F32)
    mid = r.astype(BF16)
    lo = (r - mid.astype(F32)).astype(BF16)
    return hi, mid, lo


def _mm(a, b, dims=NN):
    return _dg(a.astype(BF16), b.astype(BF16), dims)


def _mm_exact_lhs(a_bf16, b, dims=NN):
    h, m, l = _split3(b)
    return _dg(a_bf16, h, dims) + (_dg(a_bf16, m, dims) + _dg(a_bf16, l, dims))


def _mm_exact_rhs(a, b_bf16, dims=NN):
    h, m, l = _split3(a)
    return _dg(h, b_bf16, dims) + (_dg(m, b_bf16, dims) + _dg(l, b_bf16, dims))


def _sigmoid(x):
    return 1.0 / (1.0 + jnp.exp(-x))


def _silu(x):
    return x * _sigmoid(x)


def _softplus(x):
    return jnp.maximum(x, 0.0) + jnp.log(1.0 + jnp.exp(-jnp.abs(x)))


def _shift_rows(x, tail, s):
    rolled = pltpu.roll(x, s, 0)
    row = lax.broadcasted_iota(jnp.int32, (SUBLANES, 1), 0)
    top = jnp.where(row < s, pltpu.roll(tail, s, 0), rolled[:SUBLANES])
    return jnp.concatenate([top, rolled[SUBLANES:]], axis=0)


def _rows_per_step(lp):
    return CHUNK * _largest_divisor(lp // CHUNK, CHUNKS_PER_STEP, 1)


def _largest_divisor(n, cap, mult=8):
    best = None
    for d in range(mult, cap + 1, mult):
        if n % d == 0:
            best = d
    assert best is not None
    return best


def _norm_matmul_kernel(h_ref, g_ref, w_ref, o_ref, u_ref, *, tm, lp, pad):
    i = pl.program_id(0)

    @pl.when(pl.program_id(1) == 0)
    def _():
        x = h_ref[...]
        y = x * lax.rsqrt(jnp.mean(x * x, axis=-1, keepdims=True) + EPS) * g_ref[...]
        pos = (i % (lp // tm)) * tm + lax.broadcasted_iota(jnp.int32, (tm, 1), 0)
        u_ref[...] = jnp.where(pos >= pad, y, 0.0).astype(BF16)

    o_ref[...] = jnp.dot(u_ref[...], w_ref[...], preferred_element_type=F32).astype(o_ref.dtype)


def _norm_matmul(h2d, g, w, *, lp, pad, tn, out_dtype):
    m, d = h2d.shape
    n = w.shape[1]
    tm = _largest_divisor(lp, NORM_MATMUL_ROWS)
    assert m % tm == 0 and n % tn == 0
    return pl.pallas_call(
        functools.partial(_norm_matmul_kernel, tm=tm, lp=lp, pad=pad),
        grid=(m // tm, n // tn),
        in_specs=[pl.BlockSpec((tm, d), lambda i, j: (i, 0)),
                  pl.BlockSpec((1, d), lambda i, j: (0, 0)),
                  pl.BlockSpec((d, tn), lambda i, j: (0, j))],
        out_specs=pl.BlockSpec((tm, tn), lambda i, j: (i, j)),
        out_shape=jax.ShapeDtypeStruct((m, n), out_dtype),
        scratch_shapes=[pltpu.VMEM((tm, d), BF16)],
        compiler_params=pltpu.CompilerParams(dimension_semantics=("parallel", "arbitrary"),
                                             vmem_limit_bytes=VMEM_LIMIT),
        name="norm_matmul",
    )(h2d, g.reshape(1, d), w)


def _ffn_up_kernel(h_ref, g_ref, w_ref, cw_ref, cb_ref, o_ref, u_ref, *, pad):
    @pl.when(pl.program_id(1) == 0)
    def _():
        x = h_ref[...]
        y = x * lax.rsqrt(jnp.mean(x * x, axis=-1, keepdims=True) + EPS) * g_ref[...]
        pos = lax.broadcasted_iota(jnp.int32, (x.shape[0], 1), 0)
        u_ref[...] = jnp.where(pos >= pad, y, 0.0).astype(BF16)

    cw = cw_ref[...]
    w = w_ref[...]
    rows = u_ref.shape[0]
    rb = _largest_divisor(rows, rows // FFN_ROW_BLOCKS, 16)
    tail = jnp.zeros((SUBLANES, w.shape[1]), F32)
    for r0 in range(0, rows, rb):
        up = jnp.dot(u_ref[r0:r0 + rb, :], w, preferred_element_type=F32)
        conv = cb_ref[...] + up * cw[FFN_CONV - 1:FFN_CONV]
        for s in range(1, FFN_CONV):
            conv = conv + _shift_rows(up, tail, s) * cw[FFN_CONV - 1 - s:FFN_CONV - s]
        tail = up[rb - SUBLANES:]
        o_ref[r0:r0 + rb, :] = (_silu(conv[:, :FFN_TILE]) * conv[:, FFN_TILE:]).astype(o_ref.dtype)


def _ffn_up(h2d, g, w, conv_w, conv_b, *, lp, pad):
    m, d = h2d.shape
    assert m % lp == 0
    return pl.pallas_call(
        functools.partial(_ffn_up_kernel, pad=pad),
        grid=(m // lp, FFN_HIDDEN // FFN_TILE),
        in_specs=[pl.BlockSpec((lp, d), lambda i, j: (i, 0)),
                  pl.BlockSpec((1, d), lambda i, j: (0, 0)),
                  pl.BlockSpec((d, 2 * FFN_TILE), lambda i, j: (0, j)),
                  pl.BlockSpec((8, 2 * FFN_TILE), lambda i, j: (0, j)),
                  pl.BlockSpec((1, 2 * FFN_TILE), lambda i, j: (0, j))],
        out_specs=pl.BlockSpec((lp, FFN_TILE), lambda i, j: (i, j)),
        out_shape=jax.ShapeDtypeStruct((m, FFN_HIDDEN), BF16),
        scratch_shapes=[pltpu.VMEM((lp, d), BF16)],
        compiler_params=pltpu.CompilerParams(dimension_semantics=("parallel", "arbitrary"),
                                             vmem_limit_bytes=VMEM_LIMIT),
        name="ffn_up_conv_gate",
    )(h2d, g.reshape(1, d), w, conv_w, conv_b)


def _interleave_gate_val(t):
    lead = t.shape[:-1]
    t = t.reshape(lead + (2, FFN_HIDDEN // FFN_TILE, FFN_TILE))
    return jnp.swapaxes(t, -3, -2).reshape(lead + (2 * FFN_HIDDEN,))


def _unit_lower_inverse(ns, eye, same16, same32):
    ps = [jnp.where(same16, n, 0.0) for n in ns]
    ts = [eye + p for p in ps]
    for _ in range(3):
        ps = [_mm(p, p) for p in ps]
        ts = [t + _mm(t, p) for t, p in zip(ts, ps)]
    offs = [jnp.where(same32, n, 0.0) - jnp.where(same16, n, 0.0) for n in ns]
    tmp = [_mm(t, o) for t, o in zip(ts, offs)]
    ts = [t + _mm(x, t) for t, x in zip(ts, tmp)]
    offs = [jnp.where(same32, 0.0, n) for n in ns]
    tmp = [_mm(t, o) for t, o in zip(ts, offs)]
    return [t + _mm(x, t) for t, x in zip(ts, tmp)]


def _rwkv_kernel(*refs, heads, has_vres):
    it = iter(refs)
    pr_ref, pk_ref, pv_ref, pwa_ref, pgl_ref = (next(it) for _ in range(5))
    if has_vres:
        pvr_ref, vf_ref = next(it), next(it)
    pvec_ref, lmu_ref, wwa_ref, g2_ref = (next(it) for _ in range(4))
    if has_vres:
        v2_ref = next(it)
    y_ref = next(it)
    if not has_vres:
        vfo_ref = next(it)
    prev_r, prev_k, prev_v, prev_wa, prev_gl = (next(it) for _ in range(5))
    if has_vres:
        prev_vr = next(it)
    st_ref = next(it)

    ck = CHUNK
    gw = heads * A_HEAD
    tails = (prev_r, prev_k, prev_v, prev_wa, prev_gl) + ((prev_vr,) if has_vres else ())

    @pl.when(pl.program_id(2) == 0)
    def _():
        for ref in tails:
            ref[...] = jnp.zeros(ref.shape, F32)
        st_ref[...] = jnp.zeros(st_ref.shape, F32)

    pvec = pvec_ref[...]
    mu_r, mu_k, mu_v, w0, a0, k_k, k_a, r_k, ln_g, ln_b, v0 = (pvec[i:i + 1] for i in range(11))
    lmu = lmu_ref[...]
    lane = lax.broadcasted_iota(jnp.int32, (1, LANES), 1)
    ri = lax.broadcasted_iota(jnp.int32, (ck, ck), 0)
    ci = lax.broadcasted_iota(jnp.int32, (ck, ck), 1)
    strict = ri > ci
    same16 = (ri // 16) == (ci // 16)
    same32 = (ri // 32) == (ci // 32)
    eye = jnp.where(ri == ci, 1.0, 0.0).astype(F32)
    tril = jnp.where(ri >= ci, 1.0, 0.0).astype(BF16)
    incl = ri >= ci
    hsl = [slice(h * A_HEAD, (h + 1) * A_HEAD) for h in range(heads)]

    def chunk(i, carry):
        rows = pl.ds(pl.multiple_of(i * ck, ck), ck)

        def tshift(x_ref, tail_ref, mu):
            x = x_ref[0, rows, :].astype(F32)
            prev = _shift_rows(x, tail_ref[...], 1)
            tail_ref[...] = x[ck - SUBLANES:]
            return x + (prev - x) * mu

        r = tshift(pr_ref, prev_r, mu_r)
        k = tshift(pk_ref, prev_k, mu_k)
        v = tshift(pv_ref, prev_v, mu_v)
        wa = tshift(pwa_ref, prev_wa, lmu[0:1])
        gl = tshift(pgl_ref, prev_gl, lmu[1:2])

        wa_act = jnp.where(lane < A_DECAY_LORA, jnp.tanh(wa), wa)
        wa_out = _mm(wa_act, wwa_ref[0])
        w_log = -_softplus(-(w0 + wa_out[:, :gw])) - 0.5
        lw = -jnp.exp(w_log)
        a = _sigmoid(a0 + wa_out[:, gw:])
        g = _mm(_sigmoid(gl), g2_ref[...])
        if has_vres:
            vr = tshift(pvr_ref, prev_vr, lmu[2:3])
            v = v + (vf_ref[0, rows, :] - v) * _sigmoid(v0 + _mm(vr, v2_ref[...]))
        else:
            vfo_ref[0, rows, :] = v

        cum = _mm_exact_lhs(tril, lw)
        cum_last = cum[ck - 1:ck, :]
        e_neg = jnp.exp(-cum)
        e_end = jnp.exp(cum_last - cum)
        w_end = jnp.exp(cum_last)

        kk_raw = k * k_k
        kka = kk_raw * a
        kh = k * (1.0 + (a - 1.0) * k_a)
        rt_all = r * jnp.exp(cum)
        at_all = -(kk_raw * jnp.exp(cum - lw))
        khat_all = kh * e_neg
        bhat_all = kka * e_neg
        kt_all = kh * e_end
        bt_all = kka * e_end
        rk_all = r * kh * r_k

        def head_group(hids):
            hsl_ = [hsl[h] for h in hids]
            nrm = [lax.rsqrt(jnp.maximum(jnp.sum(kk_raw[:, hs] * kk_raw[:, hs], axis=-1, keepdims=True), 1e-24))
                   for hs in hsl_]
            vh = [v[:, hs] for hs in hsl_]
            s0 = [st_ref[h] for h in hids]
            at = [at_all[:, hs] * n for hs, n in zip(hsl_, nrm)]
            bhat = [bhat_all[:, hs] * n for hs, n in zip(hsl_, nrm)]
            bt = [bt_all[:, hs] * n for hs, n in zip(hsl_, nrm)]
            ar = [jnp.concatenate([x, rt_all[:, hs]], axis=0) for x, hs in zip(at, hsl_)]
            mk = [_mm(x, khat_all[:, hs], NT) for x, hs in zip(ar, hsl_)]
            mb = [_mm(x, y, NT) for x, y in zip(ar, bhat)]
            a_ak = [jnp.where(strict, x[:ck], 0.0) for x in mk]
            a_rk = [jnp.where(incl, x[ck:], 0.0) for x in mk]
            a_ab = [jnp.where(strict, x[:ck], 0.0) for x in mb]
            a_rb = [jnp.where(incl, x[ck:], 0.0) for x in mb]
            tinv = _unit_lower_inverse(a_ab, eye, same16, same32)
            ars = [_mm(x, s, NT) for x, s in zip(ar, s0)]
            rhs = [x[:ck] + _mm(m_, v_) for x, m_, v_ in zip(ars, a_ak, vh)]
            u = [_mm(t, x) for t, x in zip(tinv, rhs)]
            ys = [x[ck:] + _mm(m_, v_) + _mm(n_, u_) for x, m_, v_, n_, u_ in zip(ars, a_rk, vh, a_rb, u)]
            res = []
            for n_, h in enumerate(hids):
                hs = hsl[h]
                st_ref[h] = s0[n_] * w_end[:, hs] + _mm(vh[n_], kt_all[:, hs], TN) + _mm(u[n_], bt[n_], TN)
                yc = ys[n_] - jnp.mean(ys[n_], axis=-1, keepdims=True)
                yn = yc * lax.rsqrt(jnp.mean(yc * yc, axis=-1, keepdims=True) + A_LN_EPS)
                bonus = jnp.sum(rk_all[:, hs], axis=-1, keepdims=True) * vh[n_]
                res.append((yn * ln_g[:, hs] + ln_b[:, hs] + bonus) * g[:, hs])
            return res

        outs = head_group(list(range(heads)))
        y_ref[0, rows, :] = jnp.concatenate(outs, axis=-1).astype(y_ref.dtype)
        return carry

    lax.fori_loop(0, pr_ref.shape[1] // ck, chunk, 0)


def _rwkv_mix(proj, v_first, pvec, lmu, wwa, g2, v2, *, has_vres):
    bsz, lp, _ = proj.shape
    heads = RWKV_HEADS_PER_STEP
    gw = heads * A_HEAD
    ngrp = A_WIDTH // gw
    tr = _rows_per_step(lp)

    def col(off, width):
        return pl.BlockSpec((1, tr, width), lambda b, hg, s, o=off // width: (b, s, o))

    def colg(off):
        return pl.BlockSpec((1, tr, gw), lambda b, hg, s, o=off // gw: (b, s, o + hg))

    act_g = pl.BlockSpec((1, tr, gw), lambda b, hg, s: (b, s, hg))
    in_specs = [colg(COL_A_R), colg(COL_A_K), colg(COL_A_V), col(COL_A_WA, LANES), col(COL_A_GL, LANES)]
    args = [proj] * 5
    if has_vres:
        in_specs += [col(COL_A_VR, LANES), act_g]
        args += [proj, v_first]
    in_specs += [pl.BlockSpec((16, gw), lambda b, hg, s: (0, hg)),
                 pl.BlockSpec((8, LANES), lambda b, hg, s: (0, 0)),
                 pl.BlockSpec((1, LANES, 2 * gw), lambda b, hg, s: (hg, 0, 0)),
                 pl.BlockSpec((A_GATE_LORA, gw), lambda b, hg, s: (0, hg))]
    args += [pvec, lmu, wwa, g2]
    if has_vres:
        in_specs.append(pl.BlockSpec((LANES, gw), lambda b, hg, s: (0, hg)))
        args.append(v2)
    y_shape = jax.ShapeDtypeStruct((bsz, lp, A_WIDTH), BF16)
    if has_vres:
        out_shape, out_specs = y_shape, act_g
    else:
        out_shape = (y_shape, jax.ShapeDtypeStruct((bsz, lp, A_WIDTH), F32))
        out_specs = (act_g, act_g)
    scratch = ([pltpu.VMEM((SUBLANES, gw), F32)] * 3
               + [pltpu.VMEM((SUBLANES, LANES), F32)] * (3 if has_vres else 2))
    scratch.append(pltpu.VMEM((heads, A_HEAD, A_HEAD), F32))
    out = pl.pallas_call(
        functools.partial(_rwkv_kernel, heads=heads, has_vres=has_vres),
        grid=(bsz, ngrp, lp // tr),
        in_specs=in_specs, out_specs=out_specs, out_shape=out_shape,
        scratch_shapes=scratch,
        compiler_params=pltpu.CompilerParams(dimension_semantics=("parallel", "parallel", "arbitrary"),
                                             vmem_limit_bytes=VMEM_LIMIT),
        name="rwkv7_vres" if has_vres else "rwkv7",
    )(*args)
    return (out, v_first) if has_vres else out


def _ssd_kernel(z_ref, xbc_ref, dt_ref, cw_ref, cb_ref, hp_ref, cp_ref, e_ref, y_ref, tail_ref, st_ref, *, pad):
    ck = CHUNK
    gp = B_HEADS_PER_GROUP * B_HEAD
    nsub = xbc_ref.shape[1] // ck
    step = pl.program_id(1)

    @pl.when(step == 0)
    def _():
        tail_ref[...] = jnp.zeros(tail_ref.shape, F32)
        st_ref[...] = jnp.zeros(st_ref.shape, F32)

    row = lax.broadcasted_iota(jnp.int32, (ck, 1), 0)
    ri = lax.broadcasted_iota(jnp.int32, (ck, ck), 0)
    ci = lax.broadcasted_iota(jnp.int32, (ck, ck), 1)
    causal = ri >= ci
    tril = jnp.where(causal, 1.0, 0.0).astype(BF16)
    triu = jnp.where(ri <= ci, 1.0, 0.0).astype(BF16)
    cw = cw_ref[...]
    hp = hp_ref[...]
    cp = cp_ref[...]
    d_skip, norm_g = cp[0:1], cp[1:2]

    def chunk(i, carry):
        rows = pl.ds(pl.multiple_of(i * ck, ck), ck)
        x_raw = xbc_ref[0, rows, :].astype(F32)
        tail = tail_ref[...]
        conv = cb_ref[...] + x_raw * cw[B_CONV - 1:B_CONV]
        for s in range(1, B_CONV):
            conv = conv + _shift_rows(x_raw, tail, s) * cw[B_CONV - 1 - s:B_CONV - s]
        tail_ref[...] = x_raw[ck - SUBLANES:]
        xa = _silu(conv)
        valid = ((step * nsub + i) * ck + row) >= pad
        xs = jnp.where(valid, xa[:, :B_WIDTH], 0.0)
        bm = xa[:, B_WIDTH:B_WIDTH + B_GROUPS * B_STATE]
        cm = xa[:, B_WIDTH + B_GROUPS * B_STATE:]

        dt = _softplus(dt_ref[0, rows, :].astype(F32) + hp[0:1])
        a = dt * (-jnp.exp(hp[1:2]))
        a_cs = _mm_exact_lhs(tril, a)
        a_cs_t = _mm_exact_rhs(a, triu, TN)
        e = e_ref[...]
        dt_e = _mm_exact_rhs(dt, e)
        acs_e = _mm_exact_rhs(a_cs, e)
        atot_e = acs_e[ck - 1:ck, :]
        xdt = xs * dt_e
        xw = xdt * jnp.exp(atot_e - acs_e)
        e_in = jnp.exp(acs_e)
        e_tot = jnp.exp(atot_e)

        z = z_ref[0, rows, :].astype(F32)
        outs = []
        for g in range(B_GROUPS):
            gs = slice(g * gp, (g + 1) * gp)
            bg = bm[:, g * B_STATE:(g + 1) * B_STATE]
            cg = cm[:, g * B_STATE:(g + 1) * B_STATE]
            cb = _mm(cg, bg, NT)
            st = st_ref[:, gs]
            y_off = _mm(cg, st) * e_in[:, gs]
            y_diag = []
            for r in range(B_HEADS_PER_GROUP):
                h = g * B_HEADS_PER_GROUP + r
                hs = slice(h * B_HEAD, (h + 1) * B_HEAD)
                seg = acs_e[:, hs] - a_cs_t[h:h + 1, :]
                lmat = jnp.exp(jnp.where(causal, seg, -1e30))
                y_diag.append(_mm(cb * lmat, xdt[:, hs]))
            st_ref[:, gs] = st * e_tot[:, gs] + _mm(bg, xw[:, gs], TN)
            y = jnp.concatenate(y_diag, axis=-1) + y_off
            y = (y + xs[:, gs] * d_skip[:, gs]) * _silu(z[:, gs])
            y = y * lax.rsqrt(jnp.mean(y * y, axis=-1, keepdims=True) + EPS)
            outs.append(y * norm_g[:, gs])
        y_ref[0, rows, :] = jnp.concatenate(outs, axis=-1).astype(y_ref.dtype)
        return carry

    lax.fori_loop(0, nsub, chunk, 0)


def _ssd_mix(proj, conv_w, conv_b, head_params, chan_params, expand, *, pad):
    bsz, lp, _ = proj.shape
    tr = _rows_per_step(lp)

    def col(off, width):
        return pl.BlockSpec((1, tr, width), lambda b, s, o=off // width: (b, s, o))

    def full(shape):
        return pl.BlockSpec(shape, lambda b, s: (0,) * len(shape))

    return pl.pallas_call(
        functools.partial(_ssd_kernel, pad=pad),
        grid=(bsz, lp // tr),
        in_specs=[col(COL_Z, B_WIDTH), col(COL_XBC, B_CONV_DIM), col(COL_DT, LANES),
                  full((8, B_CONV_DIM)), full((1, B_CONV_DIM)), full((8, LANES)), full((8, B_WIDTH)),
                  full((LANES, B_WIDTH))],
        out_specs=pl.BlockSpec((1, tr, B_WIDTH), lambda b, s: (b, s, 0)),
        out_shape=jax.ShapeDtypeStruct((bsz, lp, B_WIDTH), BF16),
        scratch_shapes=[pltpu.VMEM((SUBLANES, B_CONV_DIM), F32), pltpu.VMEM((B_STATE, B_WIDTH), F32)],
        compiler_params=pltpu.CompilerParams(dimension_semantics=("parallel", "arbitrary"),
                                             vmem_limit_bytes=VMEM_LIMIT),
        name="ssd",
    )(proj, proj, proj, conv_w, conv_b, head_params, chan_params, expand)


def _retention_kernel(q_ref, k_ref, v_ref, g_ref, cos_ref, sin_ref, dec_ref, y_ref, st_ref):
    ck = CHUNK

    @pl.when(pl.program_id(1) == 0)
    def _():
        st_ref[...] = jnp.zeros(st_ref.shape, F32)

    even = (lax.broadcasted_iota(jnp.int32, (1, C_QK), 1) % 2) == 0
    hr = range(C_HEADS)

    def chunk(i, carry):
        rows = pl.ds(pl.multiple_of(i * ck, ck), ck)
        cos_t = cos_ref[rows, :]
        sin_t = sin_ref[rows, :]

        def rot(t):
            partner = jnp.where(even, pltpu.roll(t, C_QK - 1, 1), pltpu.roll(t, 1, 1))
            return t * cos_t + partner * sin_t

        q = rot(q_ref[0, rows, :].astype(F32))
        k = rot(k_ref[0, rows, :].astype(F32)) * (C_QK_HEAD ** -0.5)
        v = v_ref[0, rows, :]
        gate = _silu(g_ref[0, rows, :].astype(F32))

        qh = [q[:, h * C_QK_HEAD:(h + 1) * C_QK_HEAD] for h in hr]
        kh = [k[:, h * C_QK_HEAD:(h + 1) * C_QK_HEAD] for h in hr]
        vh = [v[:, h * C_V_HEAD:(h + 1) * C_V_HEAD].astype(BF16) for h in hr]
        s0 = [st_ref[h] for h in hr]
        scores = [_mm(qh[h], kh[h], NT) * dec_ref[3 * h] for h in hr]
        y = [_dg(scores[h].astype(BF16), vh[h], NN) + _mm(qh[h] * dec_ref[3 * h + 1], s0[h]) for h in hr]
        for h in hr:
            chunk_decay = (1.0 - 2.0 ** (-5.0 - h)) ** ck
            st_ref[h] = s0[h] * chunk_decay + _dg((kh[h] * dec_ref[3 * h + 2]).astype(BF16), vh[h], TN)
        yc = [t - jnp.mean(t, axis=-1, keepdims=True) for t in y]
        outs = [t * lax.rsqrt(jnp.mean(t * t, axis=-1, keepdims=True) + EPS) for t in yc]
        y_ref[0, rows, :] = (gate * jnp.concatenate(outs, axis=-1)).astype(y_ref.dtype)
        return carry

    lax.fori_loop(0, q_ref.shape[1] // ck, chunk, 0)


def _retention_decay_tables():
    idx = np.arange(CHUNK, dtype=np.float64)
    rel = idx[:, None] - idx[None, :]
    tabs = []
    for h in range(C_HEADS):
        log_gamma = math.log(1.0 - 2.0 ** (-5.0 - h))
        tabs.append(np.where(rel >= 0, np.exp(np.maximum(rel, 0.0) * log_gamma), 0.0))
        tabs.append(np.broadcast_to(np.exp((idx + 1.0) * log_gamma)[:, None], (CHUNK, C_QK_HEAD)))
        tabs.append(np.broadcast_to(np.exp((CHUNK - 1.0 - idx) * log_gamma)[:, None], (CHUNK, C_QK_HEAD)))
    return jnp.asarray(np.stack(tabs), F32)


def _retention_mix(proj, cos_t, sin_t):
    bsz, lp, _ = proj.shape
    tr = _rows_per_step(lp)

    def col(off, width):
        return pl.BlockSpec((1, tr, width), lambda b, s, o=off // width: (b, s, o))

    tab = pl.BlockSpec((tr, C_QK), lambda b, s: (s, 0))
    return pl.pallas_call(
        _retention_kernel,
        grid=(bsz, lp // tr),
        in_specs=[col(COL_RET_Q, C_QK), col(COL_RET_K, C_QK), col(COL_RET_V, C_WIDTH),
                  col(COL_RET_G, C_WIDTH), tab, tab,
                  pl.BlockSpec((3 * C_HEADS, CHUNK, CHUNK), lambda b, s: (0, 0, 0))],
        out_specs=pl.BlockSpec((1, tr, C_WIDTH), lambda b, s: (b, s, 0)),
        out_shape=jax.ShapeDtypeStruct((bsz, lp, C_WIDTH), BF16),
        scratch_shapes=[pltpu.VMEM((C_HEADS, C_QK_HEAD, C_V_HEAD), F32)],
        compiler_params=pltpu.CompilerParams(dimension_semantics=("parallel", "arbitrary"),
                                             vmem_limit_bytes=VMEM_LIMIT),
        name="retention",
    )(proj, proj, proj, proj, cos_t, sin_t, _retention_decay_tables())


def _merge_kernel(ya_ref, yb_ref, yc_ref, g0_ref, g1_ref, g2_ref, gb_ref, h_ref,
                  wa_ref, wb_ref, wc_ref, wo_ref, o_ref):
    gb = gb_ref[...]
    d = D_MODEL
    merged = (_sigmoid(g0_ref[...].astype(F32) + gb[:, :d]) * _dg(ya_ref[...], wa_ref[...], NN)
              + _sigmoid(g1_ref[...].astype(F32) + gb[:, d:2 * d]) * _dg(yb_ref[...], wb_ref[...], NN)
              + _sigmoid(g2_ref[...].astype(F32) + gb[:, 2 * d:]) * _dg(yc_ref[...], wc_ref[...], NN))
    o_ref[...] = h_ref[...] + _dg(merged.astype(BF16), wo_ref[...], NN)


def _merge(ya, yb, yc, proj2d, gate_bias, h2d, wba, wbb, wbc, wo):
    m, d = h2d.shape
    tm = _largest_divisor(m, 352)

    def rows(width):
        return pl.BlockSpec((tm, width), lambda i: (i, 0))

    def gcol(j):
        return pl.BlockSpec((tm, d), lambda i, o=COL_GATE // d + j: (i, o))

    def full(shape):
        return pl.BlockSpec(shape, lambda i: (0, 0))

    return pl.pallas_call(
        _merge_kernel,
        grid=(m // tm,),
        in_specs=[rows(A_WIDTH), rows(B_WIDTH), rows(C_WIDTH), gcol(0), gcol(1), gcol(2),
                  full((1, GATE_IN)), rows(d),
                  full((A_WIDTH, d)), full((B_WIDTH, d)), full((C_WIDTH, d)), full((d, d))],
        out_specs=rows(d),
        out_shape=jax.ShapeDtypeStruct((m, d), F32),
        compiler_params=pltpu.CompilerParams(dimension_semantics=("parallel",),
                                             vmem_limit_bytes=VMEM_LIMIT),
        name="merge",
    )(ya, yb, yc, proj2d, proj2d, proj2d, gate_bias, h2d, wba, wbb, wbc, wo)


def _matmul_residual_kernel(a_ref, w_ref, h_ref, o_ref):
    o_ref[...] = h_ref[...] + _dg(a_ref[...], w_ref[...], NN)


def _matmul_residual(a, w, h2d):
    m, d = h2d.shape
    kdim = a.shape[1]
    tm = _largest_divisor(m, 1056)
    return pl.pallas_call(
        _matmul_residual_kernel,
        grid=(m // tm,),
        in_specs=[pl.BlockSpec((tm, kdim), lambda i: (i, 0)), pl.BlockSpec((kdim, d), lambda i: (0, 0)),
                  pl.BlockSpec((tm, d), lambda i: (i, 0))],
        out_specs=pl.BlockSpec((tm, d), lambda i: (i, 0)),
        out_shape=jax.ShapeDtypeStruct((m, d), F32),
        compiler_params=pltpu.CompilerParams(dimension_semantics=("parallel",),
                                             vmem_limit_bytes=VMEM_LIMIT),
        name="ffn_down",
    )(a, w, h2d)


def _final_norm_kernel(h_ref, g_ref, o_ref, *, skip):
    x = h_ref[0, skip:, :]
    o_ref[0] = x * lax.rsqrt(jnp.mean(x * x, axis=-1, keepdims=True) + EPS) * g_ref[...]


def _final_norm(h3d, g, *, skip, seq):
    bsz, lp, d = h3d.shape
    assert lp == skip + seq
    return pl.pallas_call(
        functools.partial(_final_norm_kernel, skip=skip),
        grid=(bsz,),
        in_specs=[pl.BlockSpec((1, lp, d), lambda b: (b, 0, 0)), pl.BlockSpec((1, d), lambda b: (0, 0))],
        out_specs=pl.BlockSpec((1, seq, d), lambda b: (b, 0, 0)),
        out_shape=jax.ShapeDtypeStruct((bsz, seq, d), F32),
        compiler_params=pltpu.CompilerParams(dimension_semantics=("parallel",),
                                             vmem_limit_bytes=VMEM_LIMIT),
        name="final_norm",
    )(h3d, g.reshape(1, d))


def _pad_cols(t, width):
    return jnp.pad(t, [(0, 0)] * (t.ndim - 1) + [(0, width - t.shape[-1])])


def _pad_rows(t, rows):
    return jnp.pad(t, [(0, rows - t.shape[0])] + [(0, 0)] * (t.ndim - 1))


def _pack_in_proj(w, w_vres):
    a, b, c, g = (w[:, :A_IN], w[:, A_IN:A_IN + B_IN], w[:, A_IN + B_IN:A_IN + B_IN + C_IN],
                  w[:, A_IN + B_IN + C_IN:])
    aw = 3 * A_WIDTH
    parts = [b[:, B_WIDTH:B_WIDTH + B_CONV_DIM], g, b[:, :B_WIDTH],
             c[:, 2 * C_QK:2 * C_QK + C_WIDTH], c[:, 2 * C_QK + C_WIDTH:], c[:, :C_QK], c[:, C_QK:2 * C_QK],
             a[:, :aw], a[:, aw:aw + LANES], a[:, aw + LANES:],
             _pad_cols(w_vres, LANES), _pad_cols(b[:, B_WIDTH + B_CONV_DIM:], LANES)]
    out = jnp.concatenate(parts, axis=1)
    assert out.shape[1] == N_PROJ
    return out.astype(BF16)


def kernel(x, meta, norm_mix, norm_ffn, norm_final, w_in, w_in_vres, gate_bias, rwkv_mu, rwkv_mu_vres, rwkv_w0, rwkv_w2, rwkv_a0, rwkv_a2, rwkv_v0, rwkv_v2, rwkv_g2, rwkv_k_k, rwkv_k_a, rwkv_r_k, rwkv_ln_g, rwkv_ln_b, ssm_conv_w, ssm_conv_b, ssm_dt_bias, ssm_a_log, ssm_d, ssm_norm_g, w_branch, w_out, ffn_w_up, ffn_conv_w, ffn_conv_b, ffn_w_down):
    bsz, seq, d = x.shape
    depth = w_in.shape[0]
    length = N_META + seq
    pad = (-length) % CHUNK
    lp = pad + length
    assert (pad + N_META) % CHUNK == 0 and seq % CHUNK == 0
    m = bsz * lp

    h = jnp.concatenate([jnp.zeros((bsz, pad, d), F32),
                         jnp.broadcast_to(meta.astype(F32), (bsz, N_META, d)), x.astype(F32)], axis=1)
    h = h.reshape(m, d)

    pos = jnp.arange(length, dtype=F32)
    inv_freq = 1.0 / (ROPE_BASE ** jnp.linspace(0.0, 1.0, C_QK_HEAD // 2, dtype=F32))
    ang = pos[:, None] * inv_freq[None, :]
    cos_t = jnp.tile(jnp.repeat(jnp.cos(ang), 2, axis=1), (1, C_HEADS))
    sin_t = jnp.tile((jnp.repeat(jnp.sin(ang), 2, axis=1)
                      * jnp.tile(jnp.array([-1.0, 1.0], F32), C_QK_HEAD // 2)), (1, C_HEADS))
    cos_t = jnp.pad(cos_t, ((pad, 0), (0, 0)))
    sin_t = jnp.pad(sin_t, ((pad, 0), (0, 0)))

    expand = (jnp.arange(LANES)[:, None] == (jnp.arange(B_WIDTH)[None, :] // B_HEAD)).astype(BF16)

    heads = RWKV_HEADS_PER_STEP
    gw = heads * A_HEAD
    ngrp = A_WIDTH // gw
    aw = 3 * A_WIDTH
    v_first = None
    for l in range(depth):
        has_vres = l > 0
        w_vres = w_in_vres[l - 1] if has_vres else jnp.zeros((d, A_VRES_LORA), F32)
        proj = _norm_matmul(h, norm_mix[l], _pack_in_proj(w_in[l], w_vres), lp=lp, pad=pad, tn=512,
                            out_dtype=BF16)
        proj3 = proj.reshape(bsz, lp, N_PROJ)

        mu = rwkv_mu[l]
        zeros_w = jnp.zeros((A_WIDTH,), F32)
        pvec = jnp.stack([mu[:A_WIDTH], mu[A_WIDTH:2 * A_WIDTH], mu[2 * A_WIDTH:aw],
                          rwkv_w0[l], rwkv_a0[l], rwkv_k_k[l], rwkv_k_a[l], rwkv_r_k[l].reshape(-1),
                          rwkv_ln_g[l], rwkv_ln_b[l], rwkv_v0[l - 1] if has_vres else zeros_w]
                         + [zeros_w] * 5)
        mu_vr = _pad_cols(rwkv_mu_vres[l - 1], LANES) if has_vres else jnp.zeros((LANES,), F32)
        lmu = _pad_rows(jnp.stack([mu[aw:aw + LANES], mu[aw + LANES:], mu_vr]), 8)
        w2g = rwkv_w2[l].reshape(A_DECAY_LORA, ngrp, gw).transpose(1, 0, 2)
        a2g = rwkv_a2[l].reshape(A_ICLR_LORA, ngrp, gw).transpose(1, 0, 2)
        wwa = jnp.concatenate([jnp.concatenate([w2g, jnp.zeros_like(w2g)], axis=2),
                               jnp.concatenate([jnp.zeros_like(a2g), a2g], axis=2)], axis=1).astype(BF16)
        v2 = _pad_rows(rwkv_v2[l - 1], LANES).astype(BF16) if has_vres else None
        y_a, v_first = _rwkv_mix(proj3, v_first, pvec, lmu, wwa, rwkv_g2[l].astype(BF16), v2,
                                 has_vres=has_vres)

        head_params = _pad_rows(jnp.stack([_pad_cols(ssm_dt_bias[l], LANES), _pad_cols(ssm_a_log[l], LANES)]), 8)
        chan_params = _pad_rows(jnp.stack([jnp.repeat(ssm_d[l], B_HEAD), ssm_norm_g[l]]), 8)
        y_b = _ssd_mix(proj3, _pad_rows(ssm_conv_w[l], 8), ssm_conv_b[l].reshape(1, -1), head_params,
                       chan_params, expand, pad=pad)

        y_c = _retention_mix(proj3, cos_t, sin_t)

        wb = w_branch[l].astype(BF16)
        h = _merge(y_a.reshape(m, A_WIDTH), y_b.reshape(m, B_WIDTH), y_c.reshape(m, C_WIDTH), proj,
                   gate_bias[l].reshape(1, -1), h, wb[:A_WIDTH], wb[A_WIDTH:A_WIDTH + B_WIDTH],
                   wb[A_WIDTH + B_WIDTH:], w_out[l].astype(BF16))

        act = _ffn_up(h, norm_ffn[l], _interleave_gate_val(ffn_w_up[l]).astype(BF16),
                      _pad_rows(_interleave_gate_val(ffn_conv_w[l]), 8),
                      _interleave_gate_val(ffn_conv_b[l]).reshape(1, -1), lp=lp, pad=pad)
        h = _matmul_residual(act, ffn_w_down[l].astype(BF16), h)

    return _final_norm(h.reshape(bsz, lp, d), norm_final, skip=pad + N_META, seq=seq)
```

```python
import functools
import math

import numpy as np
import jax
import jax.numpy as jnp
from jax import lax
from jax.experimental import pallas as pl
from jax.experimental.pallas import tpu as pltpu

F32 = jnp.float32
BF16 = jnp.bfloat16

D_MODEL = 1024
CHUNK = 64
N_META = 16
EPS = 1e-6

A_HEAD = 64
A_WIDTH = D_MODEL
A_HEADS = A_WIDTH // A_HEAD
A_DECAY_LORA = 64
A_ICLR_LORA = 64
A_VRES_LORA = 32
A_GATE_LORA = 128
A_LN_EPS = 64e-5
A_IN = 3 * A_WIDTH + A_DECAY_LORA + A_ICLR_LORA + A_GATE_LORA

B_WIDTH = 2 * D_MODEL
B_HEAD = 64
B_HEADS = B_WIDTH // B_HEAD
B_GROUPS = 4
B_HEADS_PER_GROUP = B_HEADS // B_GROUPS
B_STATE = 128
B_CONV = 4
B_CONV_DIM = B_WIDTH + 2 * B_GROUPS * B_STATE
B_IN = B_WIDTH + B_CONV_DIM + B_HEADS

C_HEADS = 8
C_QK_HEAD = 64
C_V_HEAD = 128
C_QK = C_HEADS * C_QK_HEAD
C_WIDTH = C_HEADS * C_V_HEAD
C_IN = 2 * C_QK + 2 * C_WIDTH
ROPE_BASE = 10000.0

GATE_IN = 3 * D_MODEL
FFN_HIDDEN = 2816
FFN_CONV = 3
FFN_ROW_BLOCKS = 12
FFN_TILE = 256

LANES = 128
SUBLANES = 8

COL_XBC = 0
COL_GATE = 3072
COL_Z = 6144
COL_RET_V = 8192
COL_RET_G = 9216
COL_RET_Q = 10240
COL_RET_K = 10752
COL_A_R = 11264
COL_A_K = 12288
COL_A_V = 13312
COL_A_WA = 14336
COL_A_GL = 14464
COL_A_VR = 14592
COL_DT = 14720
N_PROJ = 14848

RWKV_HEADS_PER_STEP = 16
CHUNKS_PER_STEP = 11
VMEM_LIMIT = 48 * 1024 * 1024
NORM_MATMUL_ROWS = 2112

NN = (((1,), (0,)), ((), ()))
NT = (((1,), (1,)), ((), ()))
TN = (((0,), (0,)), ((), ()))


def _dg(a, b, dims):
    return lax.dot_general(a, b, dims, preferred_element_type=F32)


def _split3(x):
    hi = x.astype(BF16)
    r = x - hi.astype(F32)
    mid = r.astype(BF16)
    lo = (r - mid.astype(F32)).astype(BF16)
    return hi, mid, lo


def _mm(a, b, dims=NN):
    return _dg(a.astype(BF16), b.astype(BF16), dims)


def _mm_exact_lhs(a_bf16, b, dims=NN):
    h, m, l = _split3(b)
    return _dg(a_bf16, h, dims) + (_dg(a_bf16, m, dims) + _dg(a_bf16, l, dims))


def _mm_exact_rhs(a, b_bf16, dims=NN):
    h, m, l = _split3(a)
    return _dg(h, b_bf16, dims) + (_dg(m, b_bf16, dims) + _dg(l, b_bf16, dims))


def _sigmoid(x):
    return 1.0 / (1.0 + jnp.exp(-x))


def _silu(x):
    return x * _sigmoid(x)


def _softplus(x):
    return jnp.maximum(x, 0.0) + jnp.log(1.0 + jnp.exp(-jnp.abs(x)))


def _shift_rows(x, tail, s):
    rolled = pltpu.roll(x, s, 0)
    row = lax.broadcasted_iota(jnp.int32, (SUBLANES, 1), 0)
    top = jnp.where(row < s, pltpu.roll(tail, s, 0), rolled[:SUBLANES])
    return jnp.concatenate([top, rolled[SUBLANES:]], axis=0)


def _rows_per_step(lp):
    return CHUNK * _largest_divisor(lp // CHUNK, CHUNKS_PER_STEP, 1)


def _largest_divisor(n, cap, mult=8):
    best = None
    for d in range(mult, cap + 1, mult):
        if n % d == 0:
            best = d
    assert best is not None
    return best


def _norm_matmul_kernel(h_ref, g_ref, w_ref, o_ref, u_ref, *, tm, lp, pad):
    i = pl.program_id(0)

    @pl.when(pl.program_id(1) == 0)
    def _():
        x = h_ref[...]
        y = x * lax.rsqrt(jnp.mean(x * x, axis=-1, keepdims=True) + EPS) * g_ref[...]
        pos = (i % (lp // tm)) * tm + lax.broadcasted_iota(jnp.int32, (tm, 1), 0)
        u_ref[...] = jnp.where(pos >= pad, y, 0.0).astype(BF16)

    o_ref[...] = jnp.dot(u_ref[...], w_ref[...], preferred_element_type=F32).astype(o_ref.dtype)


def _norm_matmul(h2d, g, w, *, lp, pad, tn, out_dtype):
    m, d = h2d.shape
    n = w.shape[1]
    tm = _largest_divisor(lp, NORM_MATMUL_ROWS)
    assert m % tm == 0 and n % tn == 0
    return pl.pallas_call(
        functools.partial(_norm_matmul_kernel, tm=tm, lp=lp, pad=pad),
        grid=(m // tm, n // tn),
        in_specs=[pl.BlockSpec((tm, d), lambda i, j: (i, 0)),
                  pl.BlockSpec((1, d), lambda i, j: (0, 0)),
                  pl.BlockSpec((d, tn), lambda i, j: (0, j))],
        out_specs=pl.BlockSpec((tm, tn), lambda i, j: (i, j)),
        out_shape=jax.ShapeDtypeStruct((m, n), out_dtype),
        scratch_shapes=[pltpu.VMEM((tm, d), BF16)],
        compiler_params=pltpu.CompilerParams(dimension_semantics=("parallel", "arbitrary"),
                                             vmem_limit_bytes=VMEM_LIMIT),
        name="norm_matmul",
    )(h2d, g.reshape(1, d), w)


def _ffn_up_kernel(h_ref, g_ref, wg_ref, wv_ref, cwg_ref, cwv_ref, cbg_ref, cbv_ref, o_ref, u_ref, *, pad):
    @pl.when(pl.program_id(1) == 0)
    def _():
        x = h_ref[...]
        y = x * lax.rsqrt(jnp.mean(x * x, axis=-1, keepdims=True) + EPS) * g_ref[...]
        pos = lax.broadcasted_iota(jnp.int32, (x.shape[0], 1), 0)
        u_ref[...] = jnp.where(pos >= pad, y, 0.0).astype(BF16)

    rows = u_ref.shape[0]
    rb = _largest_divisor(rows, rows // FFN_ROW_BLOCKS, 16)

    def conv(up, tail, cw_ref, cb_ref):
        cw = cw_ref[...]
        out = cb_ref[...] + up * cw[FFN_CONV - 1:FFN_CONV]
        for s in range(1, FFN_CONV):
            out = out + _shift_rows(up, tail, s) * cw[FFN_CONV - 1 - s:FFN_CONV - s]
        return out

    tail_g = tail_v = jnp.zeros((SUBLANES, FFN_TILE), F32)
    for r0 in range(0, rows, rb):
        u = u_ref[r0:r0 + rb, :]
        up_g = jnp.dot(u, wg_ref[...], preferred_element_type=F32)
        up_v = jnp.dot(u, wv_ref[...], preferred_element_type=F32)
        gate = conv(up_g, tail_g, cwg_ref, cbg_ref)
        val = conv(up_v, tail_v, cwv_ref, cbv_ref)
        tail_g, tail_v = up_g[rb - SUBLANES:], up_v[rb - SUBLANES:]
        o_ref[r0:r0 + rb, :] = (_silu(gate) * val).astype(o_ref.dtype)


def _ffn_up(h2d, g, w, conv_w, conv_b, *, lp, pad):
    m, d = h2d.shape
    assert m % lp == 0
    nt = FFN_HIDDEN // FFN_TILE

    def cols(rows, off):
        return pl.BlockSpec((rows, FFN_TILE), lambda i, j, o=off: (0, j + o))

    return pl.pallas_call(
        functools.partial(_ffn_up_kernel, pad=pad),
        grid=(m // lp, nt),
        in_specs=[pl.BlockSpec((lp, d), lambda i, j: (i, 0)),
                  pl.BlockSpec((1, d), lambda i, j: (0, 0)),
                  cols(d, 0), cols(d, nt), cols(8, 0), cols(8, nt), cols(1, 0), cols(1, nt)],
        out_specs=pl.BlockSpec((lp, FFN_TILE), lambda i, j: (i, j)),
        out_shape=jax.ShapeDtypeStruct((m, FFN_HIDDEN), BF16),
        scratch_shapes=[pltpu.VMEM((lp, d), BF16)],
        compiler_params=pltpu.CompilerParams(dimension_semantics=("parallel", "arbitrary"),
                                             vmem_limit_bytes=VMEM_LIMIT),
        name="ffn_up_conv_gate",
    )(h2d, g.reshape(1, d), w, w, conv_w, conv_w, conv_b, conv_b)


def _unit_lower_inverse(ns, eye, same16, same32):
    ps = [jnp.where(same16, n, 0.0) for n in ns]
    ts = [eye + p for p in ps]
    for _ in range(3):
        ps = [_mm(p, p) for p in ps]
        ts = [t + _mm(t, p) for t, p in zip(ts, ps)]
    offs = [jnp.where(same32, n, 0.0) - jnp.where(same16, n, 0.0) for n in ns]
    tmp = [_mm(t, o) for t, o in zip(ts, offs)]
    ts = [t + _mm(x, t) for t, x in zip(ts, tmp)]
    offs = [jnp.where(same32, 0.0, n) for n in ns]
    tmp = [_mm(t, o) for t, o in zip(ts, offs)]
    return [t + _mm(x, t) for t, x in zip(ts, tmp)]


def _rwkv_kernel(*refs, heads, has_vres):
    it = iter(refs)
    pr_ref, pk_ref, pv_ref, pwa_ref, pgl_ref = (next(it) for _ in range(5))
    if has_vres:
        pvr_ref, vf_ref = next(it), next(it)
    pvec_ref, lmu_ref, wwa_ref, g2_ref = (next(it) for _ in range(4))
    if has_vres:
        v2_ref = next(it)
    y_ref = next(it)
    if not has_vres:
        vfo_ref = next(it)
    prev_r, prev_k, prev_v, prev_wa, prev_gl = (next(it) for _ in range(5))
    if has_vres:
        prev_vr = next(it)
    st_ref = next(it)

    ck = CHUNK
    gw = heads * A_HEAD
    tails = (prev_r, prev_k, prev_v, prev_wa, prev_gl) + ((prev_vr,) if has_vres else ())

    @pl.when(pl.program_id(2) == 0)
    def _():
        for ref in tails:
            ref[...] = jnp.zeros(ref.shape, F32)
        st_ref[...] = jnp.zeros(st_ref.shape, F32)

    pvec = pvec_ref[...]
    mu_r, mu_k, mu_v, w0, a0, k_k, k_a, r_k, ln_g, ln_b, v0 = (pvec[i:i + 1] for i in range(11))
    lmu = lmu_ref[...]
    lane = lax.broadcasted_iota(jnp.int32, (1, LANES), 1)
    ri = lax.broadcasted_iota(jnp.int32, (ck, ck), 0)
    ci = lax.broadcasted_iota(jnp.int32, (ck, ck), 1)
    strict = ri > ci
    same16 = (ri // 16) == (ci // 16)
    same32 = (ri // 32) == (ci // 32)
    eye = jnp.where(ri == ci, 1.0, 0.0).astype(F32)
    tril = jnp.where(ri >= ci, 1.0, 0.0).astype(BF16)
    incl = ri >= ci
    hsl = [slice(h * A_HEAD, (h + 1) * A_HEAD) for h in range(heads)]

    def chunk(i, carry):
        rows = pl.ds(pl.multiple_of(i * ck, ck), ck)

        def tshift(x_ref, tail_ref, mu):
            x = x_ref[0, rows, :].astype(F32)
            prev = _shift_rows(x, tail_ref[...], 1)
            tail_ref[...] = x[ck - SUBLANES:]
            return x + (prev - x) * mu

        r = tshift(pr_ref, prev_r, mu_r)
        k = tshift(pk_ref, prev_k, mu_k)
        v = tshift(pv_ref, prev_v, mu_v)
        wa = tshift(pwa_ref, prev_wa, lmu[0:1])
        gl = tshift(pgl_ref, prev_gl, lmu[1:2])

        wa_act = jnp.where(lane < A_DECAY_LORA, jnp.tanh(wa), wa)
        wa_out = _mm(wa_act, wwa_ref[0])
        w_log = -_softplus(-(w0 + wa_out[:, :gw])) - 0.5
        lw = -jnp.exp(w_log)
        a = _sigmoid(a0 + wa_out[:, gw:])
        g = _mm(_sigmoid(gl), g2_ref[...])
        if has_vres:
            vr = tshift(pvr_ref, prev_vr, lmu[2:3])
            v = v + (vf_ref[0, rows, :] - v) * _sigmoid(v0 + _mm(vr, v2_ref[...]))
        else:
            vfo_ref[0, rows, :] = v

        cum = _mm_exact_lhs(tril, lw)
        cum_last = cum[ck - 1:ck, :]
        e_neg = jnp.exp(-cum)
        e_end = jnp.exp(cum_last - cum)
        w_end = jnp.exp(cum_last)

        kk_raw = k * k_k
        kka = kk_raw * a
        kh = k * (1.0 + (a - 1.0) * k_a)
        rt_all = r * jnp.exp(cum)
        at_all = -(kk_raw * jnp.exp(cum - lw))
        khat_all = kh * e_neg
        bhat_all = kka * e_neg
        kt_all = kh * e_end
        bt_all = kka * e_end
        rk_all = r * kh * r_k

        def head_group(hids):
            hsl_ = [hsl[h] for h in hids]
            nrm = [lax.rsqrt(jnp.maximum(jnp.sum(kk_raw[:, hs] * kk_raw[:, hs], axis=-1, keepdims=True), 1e-24))
                   for hs in hsl_]
            vh = [v[:, hs] for hs in hsl_]
            s0 = [st_ref[h] for h in hids]
            at = [at_all[:, hs] * n for hs, n in zip(hsl_, nrm)]
            bhat = [bhat_all[:, hs] * n for hs, n in zip(hsl_, nrm)]
            bt = [bt_all[:, hs] * n for hs, n in zip(hsl_, nrm)]
            ar = [jnp.concatenate([x, rt_all[:, hs]], axis=0) for x, hs in zip(at, hsl_)]
            mk = [_mm(x, khat_all[:, hs], NT) for x, hs in zip(ar, hsl_)]
            mb = [_mm(x, y, NT) for x, y in zip(ar, bhat)]
            a_ak = [jnp.where(strict, x[:ck], 0.0) for x in mk]
            a_rk = [jnp.where(incl, x[ck:], 0.0) for x in mk]
            a_ab = [jnp.where(strict, x[:ck], 0.0) for x in mb]
            a_rb = [jnp.where(incl, x[ck:], 0.0) for x in mb]
            tinv = _unit_lower_inverse(a_ab, eye, same16, same32)
            ars = [_mm(x, s, NT) for x, s in zip(ar, s0)]
            rhs = [x[:ck] + _mm(m_, v_) for x, m_, v_ in zip(ars, a_ak, vh)]
            u = [_mm(t, x) for t, x in zip(tinv, rhs)]
            ys = [x[ck:] + _mm(m_, v_) + _mm(n_, u_) for x, m_, v_, n_, u_ in zip(ars, a_rk, vh, a_rb, u)]
            res = []
            for n_, h in enumerate(hids):
                hs = hsl[h]
                st_ref[h] = s0[n_] * w_end[:, hs] + _mm(vh[n_], kt_all[:, hs], TN) + _mm(u[n_], bt[n_], TN)
                yc = ys[n_] - jnp.mean(ys[n_], axis=-1, keepdims=True)
                yn = yc * lax.rsqrt(jnp.mean(yc * yc, axis=-1, keepdims=True) + A_LN_EPS)
                bonus = jnp.sum(rk_all[:, hs], axis=-1, keepdims=True) * vh[n_]
                res.append((yn * ln_g[:, hs] + ln_b[:, hs] + bonus) * g[:, hs])
            return res

        outs = head_group(list(range(heads)))
        y_ref[0, rows, :] = jnp.concatenate(outs, axis=-1).astype(y_ref.dtype)
        return carry

    lax.fori_loop(0, pr_ref.shape[1] // ck, chunk, 0)


def _rwkv_mix(proj, v_first, pvec, lmu, wwa, g2, v2, *, has_vres):
    bsz, lp, _ = proj.shape
    heads = RWKV_HEADS_PER_STEP
    gw = heads * A_HEAD
    ngrp = A_WIDTH // gw
    tr = _rows_per_step(lp)

    def col(off, width):
        return pl.BlockSpec((1, tr, width), lambda b, hg, s, o=off // width: (b, s, o))

    def colg(off):
        return pl.BlockSpec((1, tr, gw), lambda b, hg, s, o=off // gw: (b, s, o + hg))

    act_g = pl.BlockSpec((1, tr, gw), lambda b, hg, s: (b, s, hg))
    in_specs = [colg(COL_A_R), colg(COL_A_K), colg(COL_A_V), col(COL_A_WA, LANES), col(COL_A_GL, LANES)]
    args = [proj] * 5
    if has_vres:
        in_specs += [col(COL_A_VR, LANES), act_g]
        args += [proj, v_first]
    in_specs += [pl.BlockSpec((16, gw), lambda b, hg, s: (0, hg)),
                 pl.BlockSpec((8, LANES), lambda b, hg, s: (0, 0)),
                 pl.BlockSpec((1, LANES, 2 * gw), lambda b, hg, s: (hg, 0, 0)),
                 pl.BlockSpec((A_GATE_LORA, gw), lambda b, hg, s: (0, hg))]
    args += [pvec, lmu, wwa, g2]
    if has_vres:
        in_specs.append(pl.BlockSpec((LANES, gw), lambda b, hg, s: (0, hg)))
        args.append(v2)
    y_shape = jax.ShapeDtypeStruct((bsz, lp, A_WIDTH), BF16)
    if has_vres:
        out_shape, out_specs = y_shape, act_g
    else:
        out_shape = (y_shape, jax.ShapeDtypeStruct((bsz, lp, A_WIDTH), F32))
        out_specs = (act_g, act_g)
    scratch = ([pltpu.VMEM((SUBLANES, gw), F32)] * 3
               + [pltpu.VMEM((SUBLANES, LANES), F32)] * (3 if has_vres else 2))
    scratch.append(pltpu.VMEM((heads, A_HEAD, A_HEAD), F32))
    out = pl.pallas_call(
        functools.partial(_rwkv_kernel, heads=heads, has_vres=has_vres),
        grid=(bsz, ngrp, lp // tr),
        in_specs=in_specs, out_specs=out_specs, out_shape=out_shape,
        scratch_shapes=scratch,
        compiler_params=pltpu.CompilerParams(dimension_semantics=("parallel", "parallel", "arbitrary"),
                                             vmem_limit_bytes=VMEM_LIMIT),
        name="rwkv7_vres" if has_vres else "rwkv7",
    )(*args)
    return (out, v_first) if has_vres else out


def _ssd_kernel(z_ref, xbc_ref, dt_ref, cw_ref, cb_ref, hp_ref, cp_ref, e_ref, y_ref, tail_ref, st_ref, *, pad):
    ck = CHUNK
    gp = B_HEADS_PER_GROUP * B_HEAD
    nsub = xbc_ref.shape[1] // ck
    step = pl.program_id(1)

    @pl.when(step == 0)
    def _():
        tail_ref[...] = jnp.zeros(tail_ref.shape, F32)
        st_ref[...] = jnp.zeros(st_ref.shape, F32)

    row = lax.broadcasted_iota(jnp.int32, (ck, 1), 0)
    ri = lax.broadcasted_iota(jnp.int32, (ck, ck), 0)
    ci = lax.broadcasted_iota(jnp.int32, (ck, ck), 1)
    causal = ri >= ci
    tril = jnp.where(causal, 1.0, 0.0).astype(BF16)
    triu = jnp.where(ri <= ci, 1.0, 0.0).astype(BF16)
    cw = cw_ref[...]
    hp = hp_ref[...]
    cp = cp_ref[...]
    d_skip, norm_g = cp[0:1], cp[1:2]

    def chunk(i, carry):
        rows = pl.ds(pl.multiple_of(i * ck, ck), ck)
        x_raw = xbc_ref[0, rows, :].astype(F32)
        tail = tail_ref[...]
        conv = cb_ref[...] + x_raw * cw[B_CONV - 1:B_CONV]
        for s in range(1, B_CONV):
            conv = conv + _shift_rows(x_raw, tail, s) * cw[B_CONV - 1 - s:B_CONV - s]
        tail_ref[...] = x_raw[ck - SUBLANES:]
        xa = _silu(conv)
        valid = ((step * nsub + i) * ck + row) >= pad
        xs = jnp.where(valid, xa[:, :B_WIDTH], 0.0)
        bm = xa[:, B_WIDTH:B_WIDTH + B_GROUPS * B_STATE]
        cm = xa[:, B_WIDTH + B_GROUPS * B_STATE:]

        dt = _softplus(dt_ref[0, rows, :].astype(F32) + hp[0:1])
        a = dt * (-jnp.exp(hp[1:2]))
        a_cs = _mm_exact_lhs(tril, a)
        a_cs_t = _mm_exact_rhs(a, triu, TN)
        e = e_ref[...]
        dt_e = _mm_exact_rhs(dt, e)
        acs_e = _mm_exact_rhs(a_cs, e)
        atot_e = acs_e[ck - 1:ck, :]
        xdt = xs * dt_e
        xw = xdt * jnp.exp(atot_e - acs_e)
        e_in = jnp.exp(acs_e)
        e_tot = jnp.exp(atot_e)

        z = z_ref[0, rows, :].astype(F32)
        outs = []
        for g in range(B_GROUPS):
            gs = slice(g * gp, (g + 1) * gp)
            bg = bm[:, g * B_STATE:(g + 1) * B_STATE]
            cg = cm[:, g * B_STATE:(g + 1) * B_STATE]
            cb = _mm(cg, bg, NT)
            st = st_ref[:, gs]
            y_off = _mm(cg, st) * e_in[:, gs]
            y_diag = []
            for r in range(B_HEADS_PER_GROUP):
                h = g * B_HEADS_PER_GROUP + r
                hs = slice(h * B_HEAD, (h + 1) * B_HEAD)
                seg = acs_e[:, hs] - a_cs_t[h:h + 1, :]
                lmat = jnp.exp(jnp.where(causal, seg, -1e30))
                y_diag.append(_mm(cb * lmat, xdt[:, hs]))
            st_ref[:, gs] = st * e_tot[:, gs] + _mm(bg, xw[:, gs], TN)
            y = jnp.concatenate(y_diag, axis=-1) + y_off
            y = (y + xs[:, gs] * d_skip[:, gs]) * _silu(z[:, gs])
            y = y * lax.rsqrt(jnp.mean(y * y, axis=-1, keepdims=True) + EPS)
            outs.append(y * norm_g[:, gs])
        y_ref[0, rows, :] = jnp.concatenate(outs, axis=-1).astype(y_ref.dtype)
        return carry

    lax.fori_loop(0, nsub, chunk, 0)


def _ssd_mix(proj, conv_w, conv_b, head_params, chan_params, expand, *, pad):
    bsz, lp, _ = proj.shape
    tr = _rows_per_step(lp)

    def col(off, width):
        return pl.BlockSpec((1, tr, width), lambda b, s, o=off // width: (b, s, o))

    def full(shape):
        return pl.BlockSpec(shape, lambda b, s: (0,) * len(shape))

    return pl.pallas_call(
        functools.partial(_ssd_kernel, pad=pad),
        grid=(bsz, lp // tr),
        in_specs=[col(COL_Z, B_WIDTH), col(COL_XBC, B_CONV_DIM), col(COL_DT, LANES),
                  full((8, B_CONV_DIM)), full((1, B_CONV_DIM)), full((8, LANES)), full((8, B_WIDTH)),
                  full((LANES, B_WIDTH))],
        out_specs=pl.BlockSpec((1, tr, B_WIDTH), lambda b, s: (b, s, 0)),
        out_shape=jax.ShapeDtypeStruct((bsz, lp, B_WIDTH), BF16),
        scratch_shapes=[pltpu.VMEM((SUBLANES, B_CONV_DIM), F32), pltpu.VMEM((B_STATE, B_WIDTH), F32)],
        compiler_params=pltpu.CompilerParams(dimension_semantics=("parallel", "arbitrary"),
                                             vmem_limit_bytes=VMEM_LIMIT),
        name="ssd",
    )(proj, proj, proj, conv_w, conv_b, head_params, chan_params, expand)


def _retention_kernel(q_ref, k_ref, v_ref, g_ref, cos_ref, sin_ref, dec_ref, y_ref, st_ref):
    ck = CHUNK

    @pl.when(pl.program_id(1) == 0)
    def _():
        st_ref[...] = jnp.zeros(st_ref.shape, F32)

    even = (lax.broadcasted_iota(jnp.int32, (1, C_QK), 1) % 2) == 0
    hr = range(C_HEADS)

    def chunk(i, carry):
        rows = pl.ds(pl.multiple_of(i * ck, ck), ck)
        cos_t = cos_ref[rows, :]
        sin_t = sin_ref[rows, :]

        def rot(t):
            partner = jnp.where(even, pltpu.roll(t, C_QK - 1, 1), pltpu.roll(t, 1, 1))
            return t * cos_t + partner * sin_t

        q = rot(q_ref[0, rows, :].astype(F32))
        k = rot(k_ref[0, rows, :].astype(F32)) * (C_QK_HEAD ** -0.5)
        v = v_ref[0, rows, :]
        gate = _silu(g_ref[0, rows, :].astype(F32))

        qh = [q[:, h * C_QK_HEAD:(h + 1) * C_QK_HEAD] for h in hr]
        kh = [k[:, h * C_QK_HEAD:(h + 1) * C_QK_HEAD] for h in hr]
        vh = [v[:, h * C_V_HEAD:(h + 1) * C_V_HEAD].astype(BF16) for h in hr]
        s0 = [st_ref[h] for h in hr]
        scores = [_mm(qh[h], kh[h], NT) * dec_ref[3 * h] for h in hr]
        y = [_dg(scores[h].astype(BF16), vh[h], NN) + _mm(qh[h] * dec_ref[3 * h + 1], s0[h]) for h in hr]
        for h in hr:
            chunk_decay = (1.0 - 2.0 ** (-5.0 - h)) ** ck
            st_ref[h] = s0[h] * chunk_decay + _dg((kh[h] * dec_ref[3 * h + 2]).astype(BF16), vh[h], TN)
        yc = [t - jnp.mean(t, axis=-1, keepdims=True) for t in y]
        outs = [t * lax.rsqrt(jnp.mean(t * t, axis=-1, keepdims=True) + EPS) for t in yc]
        y_ref[0, rows, :] = (gate * jnp.concatenate(outs, axis=-1)).astype(y_ref.dtype)
        return carry

    lax.fori_loop(0, q_ref.shape[1] // ck, chunk, 0)


def _retention_decay_tables():
    idx = np.arange(CHUNK, dtype=np.float64)
    rel = idx[:, None] - idx[None, :]
    tabs = []
    for h in range(C_HEADS):
        log_gamma = math.log(1.0 - 2.0 ** (-5.0 - h))
        tabs.append(np.where(rel >= 0, np.exp(np.maximum(rel, 0.0) * log_gamma), 0.0))
        tabs.append(np.broadcast_to(np.exp((idx + 1.0) * log_gamma)[:, None], (CHUNK, C_QK_HEAD)))
        tabs.append(np.broadcast_to(np.exp((CHUNK - 1.0 - idx) * log_gamma)[:, None], (CHUNK, C_QK_HEAD)))
    return jnp.asarray(np.stack(tabs), F32)


def _retention_mix(proj, cos_t, sin_t):
    bsz, lp, _ = proj.shape
    tr = _rows_per_step(lp)

    def col(off, width):
        return pl.BlockSpec((1, tr, width), lambda b, s, o=off // width: (b, s, o))

    tab = pl.BlockSpec((tr, C_QK), lambda b, s: (s, 0))
    return pl.pallas_call(
        _retention_kernel,
        grid=(bsz, lp // tr),
        in_specs=[col(COL_RET_Q, C_QK), col(COL_RET_K, C_QK), col(COL_RET_V, C_WIDTH),
                  col(COL_RET_G, C_WIDTH), tab, tab,
                  pl.BlockSpec((3 * C_HEADS, CHUNK, CHUNK), lambda b, s: (0, 0, 0))],
        out_specs=pl.BlockSpec((1, tr, C_WIDTH), lambda b, s: (b, s, 0)),
        out_shape=jax.ShapeDtypeStruct((bsz, lp, C_WIDTH), BF16),
        scratch_shapes=[pltpu.VMEM((C_HEADS, C_QK_HEAD, C_V_HEAD), F32)],
        compiler_params=pltpu.CompilerParams(dimension_semantics=("parallel", "arbitrary"),
                                             vmem_limit_bytes=VMEM_LIMIT),
        name="retention",
    )(proj, proj, proj, proj, cos_t, sin_t, _retention_decay_tables())


def _merge_kernel(ya_ref, yb_ref, yc_ref, g0_ref, g1_ref, g2_ref, gb_ref, h_ref,
                  wa_ref, wb0_ref, wb1_ref, wc_ref, wo_ref, o_ref):
    gb = gb_ref[...]
    d = D_MODEL
    yb = yb_ref[...]
    branch_b = _dg(yb[:, :d], wb0_ref[...], NN) + _dg(yb[:, d:], wb1_ref[...], NN)
    merged = (_sigmoid(g0_ref[...].astype(F32) + gb[:, :d]) * _dg(ya_ref[...], wa_ref[...], NN)
              + _sigmoid(g1_ref[...].astype(F32) + gb[:, d:2 * d]) * branch_b
              + _sigmoid(g2_ref[...].astype(F32) + gb[:, 2 * d:]) * _dg(yc_ref[...], wc_ref[...], NN))
    o_ref[...] = h_ref[...] + _dg(merged.astype(BF16), wo_ref[...], NN)


def _merge(ya, yb, yc, proj2d, gate_bias, h2d, w_branch, wo):
    m, d = h2d.shape
    assert A_WIDTH == d and B_WIDTH == 2 * d and C_WIDTH == d
    tm = _largest_divisor(m, 352)

    def rows(width):
        return pl.BlockSpec((tm, width), lambda i: (i, 0))

    def gcol(j):
        return pl.BlockSpec((tm, d), lambda i, o=COL_GATE // d + j: (i, o))

    def full(shape):
        return pl.BlockSpec(shape, lambda i: (0, 0))

    def wblock(k):
        return pl.BlockSpec((d, d), lambda i, k=k: (k, 0))

    return pl.pallas_call(
        _merge_kernel,
        grid=(m // tm,),
        in_specs=[rows(A_WIDTH), rows(B_WIDTH), rows(C_WIDTH), gcol(0), gcol(1), gcol(2),
                  full((1, GATE_IN)), rows(d),
                  wblock(0), wblock(1), wblock(2), wblock(3), full((d, d))],
        out_specs=rows(d),
        out_shape=jax.ShapeDtypeStruct((m, d), F32),
        compiler_params=pltpu.CompilerParams(dimension_semantics=("parallel",),
                                             vmem_limit_bytes=VMEM_LIMIT),
        name="merge",
    )(ya, yb, yc, proj2d, proj2d, proj2d, gate_bias, h2d, w_branch, w_branch, w_branch, w_branch, wo)


def _matmul_residual_kernel(a_ref, w_ref, h_ref, o_ref):
    o_ref[...] = h_ref[...] + _dg(a_ref[...], w_ref[...], NN)


def _matmul_residual(a, w, h2d):
    m, d = h2d.shape
    kdim = a.shape[1]
    tm = _largest_divisor(m, 1056)
    return pl.pallas_call(
        _matmul_residual_kernel,
        grid=(m // tm,),
        in_specs=[pl.BlockSpec((tm, kdim), lambda i: (i, 0)), pl.BlockSpec((kdim, d), lambda i: (0, 0)),
                  pl.BlockSpec((tm, d), lambda i: (i, 0))],
        out_specs=pl.BlockSpec((tm, d), lambda i: (i, 0)),
        out_shape=jax.ShapeDtypeStruct((m, d), F32),
        compiler_params=pltpu.CompilerParams(dimension_semantics=("parallel",),
                                             vmem_limit_bytes=VMEM_LIMIT),
        name="ffn_down",
    )(a, w, h2d)


def _final_norm_kernel(h_ref, g_ref, o_ref, *, skip):
    x = h_ref[0, skip:, :]
    o_ref[0] = x * lax.rsqrt(jnp.mean(x * x, axis=-1, keepdims=True) + EPS) * g_ref[...]


def _final_norm(h3d, g, *, skip, seq):
    bsz, lp, d = h3d.shape
    assert lp == skip + seq
    return pl.pallas_call(
        functools.partial(_final_norm_kernel, skip=skip),
        grid=(bsz,),
        in_specs=[pl.BlockSpec((1, lp, d), lambda b: (b, 0, 0)), pl.BlockSpec((1, d), lambda b: (0, 0))],
        out_specs=pl.BlockSpec((1, seq, d), lambda b: (b, 0, 0)),
        out_shape=jax.ShapeDtypeStruct((bsz, seq, d), F32),
        compiler_params=pltpu.CompilerParams(dimension_semantics=("parallel",),
                                             vmem_limit_bytes=VMEM_LIMIT),
        name="final_norm",
    )(h3d, g.reshape(1, d))


def _pad_cols(t, width):
    return jnp.pad(t, [(0, 0)] * (t.ndim - 1) + [(0, width - t.shape[-1])])


def _pad_rows(t, rows):
    return jnp.pad(t, [(0, rows - t.shape[0])] + [(0, 0)] * (t.ndim - 1))


def _pack_in_proj(w, w_vres):
    a, b, c, g = (w[:, :A_IN], w[:, A_IN:A_IN + B_IN], w[:, A_IN + B_IN:A_IN + B_IN + C_IN],
                  w[:, A_IN + B_IN + C_IN:])
    aw = 3 * A_WIDTH
    parts = [b[:, B_WIDTH:B_WIDTH + B_CONV_DIM], g, b[:, :B_WIDTH],
             c[:, 2 * C_QK:2 * C_QK + C_WIDTH], c[:, 2 * C_QK + C_WIDTH:], c[:, :C_QK], c[:, C_QK:2 * C_QK],
             a[:, :aw], a[:, aw:aw + LANES], a[:, aw + LANES:],
             _pad_cols(w_vres, LANES), _pad_cols(b[:, B_WIDTH + B_CONV_DIM:], LANES)]
    out = jnp.concatenate(parts, axis=1)
    assert out.shape[1] == N_PROJ
    return out.astype(BF16)


def kernel(x, meta, norm_mix, norm_ffn, norm_final, w_in, w_in_vres, gate_bias, rwkv_mu, rwkv_mu_vres, rwkv_w0, rwkv_w2, rwkv_a0, rwkv_a2, rwkv_v0, rwkv_v2, rwkv_g2, rwkv_k_k, rwkv_k_a, rwkv_r_k, rwkv_ln_g, rwkv_ln_b, ssm_conv_w, ssm_conv_b, ssm_dt_bias, ssm_a_log, ssm_d, ssm_norm_g, w_branch, w_out, ffn_w_up, ffn_conv_w, ffn_conv_b, ffn_w_down):
    bsz, seq, d = x.shape
    depth = w_in.shape[0]
    length = N_META + seq
    pad = (-length) % CHUNK
    lp = pad + length
    assert (pad + N_META) % CHUNK == 0 and seq % CHUNK == 0
    m = bsz * lp

    h = jnp.concatenate([jnp.zeros((bsz, pad, d), F32),
                         jnp.broadcast_to(meta.astype(F32), (bsz, N_META, d)), x.astype(F32)], axis=1)
    h = h.reshape(m, d)

    pos = jnp.arange(length, dtype=F32)
    inv_freq = 1.0 / (ROPE_BASE ** jnp.linspace(0.0, 1.0, C_QK_HEAD // 2, dtype=F32))
    ang = pos[:, None] * inv_freq[None, :]
    cos_t = jnp.tile(jnp.repeat(jnp.cos(ang), 2, axis=1), (1, C_HEADS))
    sin_t = jnp.tile((jnp.repeat(jnp.sin(ang), 2, axis=1)
                      * jnp.tile(jnp.array([-1.0, 1.0], F32), C_QK_HEAD // 2)), (1, C_HEADS))
    cos_t = jnp.pad(cos_t, ((pad, 0), (0, 0)))
    sin_t = jnp.pad(sin_t, ((pad, 0), (0, 0)))

    expand = (jnp.arange(LANES)[:, None] == (jnp.arange(B_WIDTH)[None, :] // B_HEAD)).astype(BF16)

    heads = RWKV_HEADS_PER_STEP
    gw = heads * A_HEAD
    ngrp = A_WIDTH // gw
    aw = 3 * A_WIDTH
    v_first = None
    for l in range(depth):
        has_vres = l > 0
        w_vres = w_in_vres[l - 1] if has_vres else jnp.zeros((d, A_VRES_LORA), F32)
        proj = _norm_matmul(h, norm_mix[l], _pack_in_proj(w_in[l], w_vres), lp=lp, pad=pad, tn=512,
                            out_dtype=BF16)
        proj3 = proj.reshape(bsz, lp, N_PROJ)

        mu = rwkv_mu[l]
        zeros_w = jnp.zeros((A_WIDTH,), F32)
        pvec = jnp.stack([mu[:A_WIDTH], mu[A_WIDTH:2 * A_WIDTH], mu[2 * A_WIDTH:aw],
                          rwkv_w0[l], rwkv_a0[l], rwkv_k_k[l], rwkv_k_a[l], rwkv_r_k[l].reshape(-1),
                          rwkv_ln_g[l], rwkv_ln_b[l], rwkv_v0[l - 1] if has_vres else zeros_w]
                         + [zeros_w] * 5)
        mu_vr = _pad_cols(rwkv_mu_vres[l - 1], LANES) if has_vres else jnp.zeros((LANES,), F32)
        lmu = _pad_rows(jnp.stack([mu[aw:aw + LANES], mu[aw + LANES:], mu_vr]), 8)
        w2g = rwkv_w2[l].reshape(A_DECAY_LORA, ngrp, gw).transpose(1, 0, 2)
        a2g = rwkv_a2[l].reshape(A_ICLR_LORA, ngrp, gw).transpose(1, 0, 2)
        wwa = jnp.concatenate([jnp.concatenate([w2g, jnp.zeros_like(w2g)], axis=2),
                               jnp.concatenate([jnp.zeros_like(a2g), a2g], axis=2)], axis=1).astype(BF16)
        v2 = _pad_rows(rwkv_v2[l - 1], LANES).astype(BF16) if has_vres else None
        y_a, v_first = _rwkv_mix(proj3, v_first, pvec, lmu, wwa, rwkv_g2[l].astype(BF16), v2,
                                 has_vres=has_vres)

        head_params = _pad_rows(jnp.stack([_pad_cols(ssm_dt_bias[l], LANES), _pad_cols(ssm_a_log[l], LANES)]), 8)
        chan_params = _pad_rows(jnp.stack([jnp.repeat(ssm_d[l], B_HEAD), ssm_norm_g[l]]), 8)
        y_b = _ssd_mix(proj3, _pad_rows(ssm_conv_w[l], 8), ssm_conv_b[l].reshape(1, -1), head_params,
                       chan_params, expand, pad=pad)

        y_c = _retention_mix(proj3, cos_t, sin_t)

        h = _merge(y_a.reshape(m, A_WIDTH), y_b.reshape(m, B_WIDTH), y_c.reshape(m, C_WIDTH), proj,
                   gate_bias[l].reshape(1, -1), h, w_branch[l].astype(BF16), w_out[l].astype(BF16))

        act = _ffn_up(h, norm_ffn[l], ffn_w_up[l].astype(BF16), _pad_rows(ffn_conv_w[l], 8),
                      ffn_conv_b[l].reshape(1, -1), lp=lp, pad=pad)
        h = _matmul_residual(act, ffn_w_down[l].astype(BF16), h)

    return _final_norm(h.reshape(bsz, lp, d), norm_final, skip=pad + N_META, seq=seq)
```

```python
import functools
import math

import numpy as np
import jax
import jax.numpy as jnp
from jax import lax
from jax.experimental import pallas as pl
from jax.experimental.pallas import tpu as pltpu

F32 = jnp.float32
BF16 = jnp.bfloat16

D_MODEL = 1024
CHUNK = 64
N_META = 16
EPS = 1e-6

A_HEAD = 64
A_WIDTH = D_MODEL
A_HEADS = A_WIDTH // A_HEAD
A_DECAY_LORA = 64
A_ICLR_LORA = 64
A_VRES_LORA = 32
A_GATE_LORA = 128
A_LN_EPS = 64e-5
A_IN = 3 * A_WIDTH + A_DECAY_LORA + A_ICLR_LORA + A_GATE_LORA

B_WIDTH = 2 * D_MODEL
B_HEAD = 64
B_HEADS = B_WIDTH // B_HEAD
B_GROUPS = 4
B_HEADS_PER_GROUP = B_HEADS // B_GROUPS
B_STATE = 128
B_CONV = 4
B_CONV_DIM = B_WIDTH + 2 * B_GROUPS * B_STATE
B_IN = B_WIDTH + B_CONV_DIM + B_HEADS

C_HEADS = 8
C_QK_HEAD = 64
C_V_HEAD = 128
C_QK = C_HEADS * C_QK_HEAD
C_WIDTH = C_HEADS * C_V_HEAD
C_IN = 2 * C_QK + 2 * C_WIDTH
ROPE_BASE = 10000.0

GATE_IN = 3 * D_MODEL
FFN_HIDDEN = 2816
FFN_CONV = 3
FFN_ROW_BLOCKS = 12
FFN_TILE = 256

LANES = 128
SUBLANES = 8

COL_XBC = 0
COL_GATE = 3072
COL_Z = 6144
COL_RET_V = 8192
COL_RET_G = 9216
COL_RET_Q = 10240
COL_RET_K = 10752
COL_A_R = 11264
COL_A_K = 12288
COL_A_V = 13312
COL_A_WA = 14336
COL_A_GL = 14464
COL_A_VR = 14592
COL_DT = 14720
N_PROJ = 14848

RWKV_HEADS_PER_STEP = 16
CHUNKS_PER_STEP = 11
VMEM_LIMIT = 48 * 1024 * 1024
NORM_MATMUL_ROWS = 2112

NN = (((1,), (0,)), ((), ()))
NT = (((1,), (1,)), ((), ()))
TN = (((0,), (0,)), ((), ()))


def _dg(a, b, dims):
    return lax.dot_general(a, b, dims, preferred_element_type=F32)


def _split3(x):
    hi = x.astype(BF16)
    r = x - hi.astype(F32)
    mid = r.astype(BF16)
    lo = (r - mid.astype(F32)).astype(BF16)
    return hi, mid, lo


def _mm(a, b, dims=NN):
    return _dg(a.astype(BF16), b.astype(BF16), dims)


def _mm_exact_lhs(a_bf16, b, dims=NN):
    h, m, l = _split3(b)
    return _dg(a_bf16, h, dims) + (_dg(a_bf16, m, dims) + _dg(a_bf16, l, dims))


def _mm_exact_rhs(a, b_bf16, dims=NN):
    h, m, l = _split3(a)
    return _dg(h, b_bf16, dims) + (_dg(m, b_bf16, dims) + _dg(l, b_bf16, dims))


def _sigmoid(x):
    return 1.0 / (1.0 + jnp.exp(-x))


def _silu(x):
    return x * _sigmoid(x)


def _softplus(x):
    return jnp.maximum(x, 0.0) + jnp.log(1.0 + jnp.exp(-jnp.abs(x)))


def _shift_rows(x, tail, s):
    rolled = pltpu.roll(x, s, 0)
    row = lax.broadcasted_iota(jnp.int32, (SUBLANES, 1), 0)
    top = jnp.where(row < s, pltpu.roll(tail, s, 0), rolled[:SUBLANES])
    return jnp.concatenate([top, rolled[SUBLANES:]], axis=0)


def _rows_per_step(lp):
    return CHUNK * _largest_divisor(lp // CHUNK, CHUNKS_PER_STEP, 1)


def _largest_divisor(n, cap, mult=8):
    best = None
    for d in range(mult, cap + 1, mult):
        if n % d == 0:
            best = d
    assert best is not None
    return best


def _norm_matmul_kernel(h_ref, g_ref, w_ref, o_ref, u_ref, *, tm, lp, pad):
    i = pl.program_id(0)

    @pl.when(pl.program_id(1) == 0)
    def _():
        x = h_ref[...]
        y = x * lax.rsqrt(jnp.mean(x * x, axis=-1, keepdims=True) + EPS) * g_ref[...]
        pos = (i % (lp // tm)) * tm + lax.broadcasted_iota(jnp.int32, (tm, 1), 0)
        u_ref[...] = jnp.where(pos >= pad, y, 0.0).astype(BF16)

    o_ref[...] = jnp.dot(u_ref[...], w_ref[...], preferred_element_type=F32).astype(o_ref.dtype)


def _norm_matmul(h2d, g, w, *, lp, pad, tn, out_dtype):
    m, d = h2d.shape
    n = w.shape[1]
    tm = _largest_divisor(lp, NORM_MATMUL_ROWS)
    assert m % tm == 0 and n % tn == 0
    return pl.pallas_call(
        functools.partial(_norm_matmul_kernel, tm=tm, lp=lp, pad=pad),
        grid=(m // tm, n // tn),
        in_specs=[pl.BlockSpec((tm, d), lambda i, j: (i, 0)),
                  pl.BlockSpec((1, d), lambda i, j: (0, 0)),
                  pl.BlockSpec((d, tn), lambda i, j: (0, j))],
        out_specs=pl.BlockSpec((tm, tn), lambda i, j: (i, j)),
        out_shape=jax.ShapeDtypeStruct((m, n), out_dtype),
        scratch_shapes=[pltpu.VMEM((tm, d), BF16)],
        compiler_params=pltpu.CompilerParams(dimension_semantics=("parallel", "arbitrary"),
                                             vmem_limit_bytes=VMEM_LIMIT),
        name="norm_matmul",
    )(h2d, g.reshape(1, d), w)


def _ffn_up_kernel(h_ref, g_ref, wg_ref, wv_ref, cwg_ref, cwv_ref, cbg_ref, cbv_ref, o_ref, u_ref, *, pad):
    @pl.when(pl.program_id(1) == 0)
    def _():
        x = h_ref[...]
        y = x * lax.rsqrt(jnp.mean(x * x, axis=-1, keepdims=True) + EPS) * g_ref[...]
        pos = lax.broadcasted_iota(jnp.int32, (x.shape[0], 1), 0)
        u_ref[...] = jnp.where(pos >= pad, y, 0.0).astype(BF16)

    rows = u_ref.shape[0]
    rb = _largest_divisor(rows, rows // FFN_ROW_BLOCKS, 16)

    def conv(up, tail, cw_ref, cb_ref):
        cw = cw_ref[...]
        out = cb_ref[...] + up * cw[FFN_CONV - 1:FFN_CONV]
        for s in range(1, FFN_CONV):
            out = out + _shift_rows(up, tail, s) * cw[FFN_CONV - 1 - s:FFN_CONV - s]
        return out

    tail_g = tail_v = jnp.zeros((SUBLANES, FFN_TILE), F32)
    for r0 in range(0, rows, rb):
        u = u_ref[r0:r0 + rb, :]
        up_g = jnp.dot(u, wg_ref[...], preferred_element_type=F32)
        up_v = jnp.dot(u, wv_ref[...], preferred_element_type=F32)
        gate = conv(up_g, tail_g, cwg_ref, cbg_ref)
        val = conv(up_v, tail_v, cwv_ref, cbv_ref)
        tail_g, tail_v = up_g[rb - SUBLANES:], up_v[rb - SUBLANES:]
        o_ref[r0:r0 + rb, :] = (_silu(gate) * val).astype(o_ref.dtype)


def _ffn_up(h2d, g, w, conv_w, conv_b, *, lp, pad):
    m, d = h2d.shape
    assert m % lp == 0
    nt = FFN_HIDDEN // FFN_TILE

    def cols(rows, off):
        return pl.BlockSpec((rows, FFN_TILE), lambda i, j, o=off: (0, j + o))

    return pl.pallas_call(
        functools.partial(_ffn_up_kernel, pad=pad),
        grid=(m // lp, nt),
        in_specs=[pl.BlockSpec((lp, d), lambda i, j: (i, 0)),
                  pl.BlockSpec((1, d), lambda i, j: (0, 0)),
                  cols(d, 0), cols(d, nt), cols(8, 0), cols(8, nt), cols(1, 0), cols(1, nt)],
        out_specs=pl.BlockSpec((lp, FFN_TILE), lambda i, j: (i, j)),
        out_shape=jax.ShapeDtypeStruct((m, FFN_HIDDEN), BF16),
        scratch_shapes=[pltpu.VMEM((lp, d), BF16)],
        compiler_params=pltpu.CompilerParams(dimension_semantics=("parallel", "arbitrary"),
                                             vmem_limit_bytes=VMEM_LIMIT),
        name="ffn_up_conv_gate",
    )(h2d, g.reshape(1, d), w, w, conv_w, conv_w, conv_b, conv_b)


def _unit_lower_inverse(ns, eye, same16, same32):
    ps = [jnp.where(same16, n, 0.0) for n in ns]
    ts = [eye + p for p in ps]
    for _ in range(3):
        ps = [_mm(p, p) for p in ps]
        ts = [t + _mm(t, p) for t, p in zip(ts, ps)]
    offs = [jnp.where(same32, n, 0.0) - jnp.where(same16, n, 0.0) for n in ns]
    tmp = [_mm(t, o) for t, o in zip(ts, offs)]
    ts = [t + _mm(x, t) for t, x in zip(ts, tmp)]
    offs = [jnp.where(same32, 0.0, n) for n in ns]
    tmp = [_mm(t, o) for t, o in zip(ts, offs)]
    return [t + _mm(x, t) for t, x in zip(ts, tmp)]


def _rwkv_kernel(*refs, heads, has_vres):
    it = iter(refs)
    pr_ref, pk_ref, pv_ref, pwa_ref, pgl_ref = (next(it) for _ in range(5))
    if has_vres:
        pvr_ref, vf_ref = next(it), next(it)
    pvec_ref, lmu_ref, wwa_ref, g2_ref = (next(it) for _ in range(4))
    if has_vres:
        v2_ref = next(it)
    y_ref = next(it)
    if not has_vres:
        vfo_ref = next(it)
    prev_r, prev_k, prev_v, prev_wa, prev_gl = (next(it) for _ in range(5))
    if has_vres:
        prev_vr = next(it)
    st_ref = next(it)

    ck = CHUNK
    gw = heads * A_HEAD
    tails = (prev_r, prev_k, prev_v, prev_wa, prev_gl) + ((prev_vr,) if has_vres else ())

    @pl.when(pl.program_id(2) == 0)
    def _():
        for ref in tails:
            ref[...] = jnp.zeros(ref.shape, F32)
        st_ref[...] = jnp.zeros(st_ref.shape, F32)

    pvec = pvec_ref[...]
    mu_r, mu_k, mu_v, w0, a0, k_k, k_a, r_k, ln_g, ln_b, v0 = (pvec[i:i + 1] for i in range(11))
    lmu = lmu_ref[...]
    lane = lax.broadcasted_iota(jnp.int32, (1, LANES), 1)
    ri = lax.broadcasted_iota(jnp.int32, (ck, ck), 0)
    ci = lax.broadcasted_iota(jnp.int32, (ck, ck), 1)
    strict = ri > ci
    same16 = (ri // 16) == (ci // 16)
    same32 = (ri // 32) == (ci // 32)
    eye = jnp.where(ri == ci, 1.0, 0.0).astype(F32)
    tril = jnp.where(ri >= ci, 1.0, 0.0).astype(BF16)
    incl = ri >= ci
    hsl = [slice(h * A_HEAD, (h + 1) * A_HEAD) for h in range(heads)]

    def chunk(i, carry):
        rows = pl.ds(pl.multiple_of(i * ck, ck), ck)

        def tshift(x_ref, tail_ref, mu):
            x = x_ref[0, rows, :].astype(F32)
            prev = _shift_rows(x, tail_ref[...], 1)
            tail_ref[...] = x[ck - SUBLANES:]
            return x + (prev - x) * mu

        r = tshift(pr_ref, prev_r, mu_r)
        k = tshift(pk_ref, prev_k, mu_k)
        v = tshift(pv_ref, prev_v, mu_v)
        wa = tshift(pwa_ref, prev_wa, lmu[0:1])
        gl = tshift(pgl_ref, prev_gl, lmu[1:2])

        wa_act = jnp.where(lane < A_DECAY_LORA, jnp.tanh(wa), wa)
        wa_out = _mm(wa_act, wwa_ref[0])
        w_log = -_softplus(-(w0 + wa_out[:, :gw])) - 0.5
        lw = -jnp.exp(w_log)
        a = _sigmoid(a0 + wa_out[:, gw:])
        g = _mm(_sigmoid(gl), g2_ref[...])
        if has_vres:
            vr = tshift(pvr_ref, prev_vr, lmu[2:3])
            v = v + (vf_ref[0, rows, :] - v) * _sigmoid(v0 + _mm(vr, v2_ref[...]))
        else:
            vfo_ref[0, rows, :] = v

        cum = _mm_exact_lhs(tril, lw)
        cum_last = cum[ck - 1:ck, :]
        e_neg = jnp.exp(-cum)
        e_end = jnp.exp(cum_last - cum)
        w_end = jnp.exp(cum_last)

        kk_raw = k * k_k
        kka = kk_raw * a
        kh = k * (1.0 + (a - 1.0) * k_a)
        rt_all = r * jnp.exp(cum)
        at_all = -(kk_raw * jnp.exp(cum - lw))
        khat_all = kh * e_neg
        bhat_all = kka * e_neg
        kt_all = kh * e_end
        bt_all = kka * e_end
        rk_all = r * kh * r_k

        def head_group(hids):
            hsl_ = [hsl[h] for h in hids]
            nrm = [lax.rsqrt(jnp.maximum(jnp.sum(kk_raw[:, hs] * kk_raw[:, hs], axis=-1, keepdims=True), 1e-24))
                   for hs in hsl_]
            vh = [v[:, hs] for hs in hsl_]
            s0 = [st_ref[h] for h in hids]
            at = [at_all[:, hs] * n for hs, n in zip(hsl_, nrm)]
            bhat = [bhat_all[:, hs] * n for hs, n in zip(hsl_, nrm)]
            bt = [bt_all[:, hs] * n for hs, n in zip(hsl_, nrm)]
            ar = [jnp.concatenate([x, rt_all[:, hs]], axis=0) for x, hs in zip(at, hsl_)]
            mk = [_mm(x, khat_all[:, hs], NT) for x, hs in zip(ar, hsl_)]
            mb = [_mm(x, y, NT) for x, y in zip(ar, bhat)]
            a_ak = [jnp.where(strict, x[:ck], 0.0) for x in mk]
            a_rk = [jnp.where(incl, x[ck:], 0.0) for x in mk]
            a_ab = [jnp.where(strict, x[:ck], 0.0) for x in mb]
            a_rb = [jnp.where(incl, x[ck:], 0.0) for x in mb]
            tinv = _unit_lower_inverse(a_ab, eye, same16, same32)
            ars = [_mm(x, s, NT) for x, s in zip(ar, s0)]
            rhs = [x[:ck] + _mm(m_, v_) for x, m_, v_ in zip(ars, a_ak, vh)]
            u = [_mm(t, x) for t, x in zip(tinv, rhs)]
            ys = [x[ck:] + _mm(m_, v_) + _mm(n_, u_) for x, m_, v_, n_, u_ in zip(ars, a_rk, vh, a_rb, u)]
            res = []
            for n_, h in enumerate(hids):
                hs = hsl[h]
                st_ref[h] = s0[n_] * w_end[:, hs] + _mm(vh[n_], kt_all[:, hs], TN) + _mm(u[n_], bt[n_], TN)
                yc = ys[n_] - jnp.mean(ys[n_], axis=-1, keepdims=True)
                yn = yc * lax.rsqrt(jnp.mean(yc * yc, axis=-1, keepdims=True) + A_LN_EPS)
                bonus = jnp.sum(rk_all[:, hs], axis=-1, keepdims=True) * vh[n_]
                res.append((yn * ln_g[:, hs] + ln_b[:, hs] + bonus) * g[:, hs])
            return res

        outs = head_group(list(range(heads)))
        y_ref[0, rows, :] = jnp.concatenate(outs, axis=-1).astype(y_ref.dtype)
        return carry

    lax.fori_loop(0, pr_ref.shape[1] // ck, chunk, 0)


def _rwkv_mix(proj, v_first, pvec, lmu, wwa, g2, v2, *, has_vres):
    bsz, lp, _ = proj.shape
    heads = RWKV_HEADS_PER_STEP
    gw = heads * A_HEAD
    ngrp = A_WIDTH // gw
    tr = _rows_per_step(lp)

    def col(off, width):
        return pl.BlockSpec((1, tr, width), lambda b, hg, s, o=off // width: (b, s, o))

    def colg(off):
        return pl.BlockSpec((1, tr, gw), lambda b, hg, s, o=off // gw: (b, s, o + hg))

    act_g = pl.BlockSpec((1, tr, gw), lambda b, hg, s: (b, s, hg))
    in_specs = [colg(COL_A_R), colg(COL_A_K), colg(COL_A_V), col(COL_A_WA, LANES), col(COL_A_GL, LANES)]
    args = [proj] * 5
    if has_vres:
        in_specs += [col(COL_A_VR, LANES), act_g]
        args += [proj, v_first]
    in_specs += [pl.BlockSpec((16, gw), lambda b, hg, s: (0, hg)),
                 pl.BlockSpec((8, LANES), lambda b, hg, s: (0, 0)),
                 pl.BlockSpec((1, LANES, 2 * gw), lambda b, hg, s: (hg, 0, 0)),
                 pl.BlockSpec((A_GATE_LORA, gw), lambda b, hg, s: (0, hg))]
    args += [pvec, lmu, wwa, g2]
    if has_vres:
        in_specs.append(pl.BlockSpec((LANES, gw), lambda b, hg, s: (0, hg)))
        args.append(v2)
    y_shape = jax.ShapeDtypeStruct((bsz, lp, A_WIDTH), BF16)
    if has_vres:
        out_shape, out_specs = y_shape, act_g
    else:
        out_shape = (y_shape, jax.ShapeDtypeStruct((bsz, lp, A_WIDTH), F32))
        out_specs = (act_g, act_g)
    scratch = ([pltpu.VMEM((SUBLANES, gw), F32)] * 3
               + [pltpu.VMEM((SUBLANES, LANES), F32)] * (3 if has_vres else 2))
    scratch.append(pltpu.VMEM((heads, A_HEAD, A_HEAD), F32))
    out = pl.pallas_call(
        functools.partial(_rwkv_kernel, heads=heads, has_vres=has_vres),
        grid=(bsz, ngrp, lp // tr),
        in_specs=in_specs, out_specs=out_specs, out_shape=out_shape,
        scratch_shapes=scratch,
        compiler_params=pltpu.CompilerParams(dimension_semantics=("parallel", "parallel", "arbitrary"),
                                             vmem_limit_bytes=VMEM_LIMIT),
        name="rwkv7_vres" if has_vres else "rwkv7",
    )(*args)
    return (out, v_first) if has_vres else out


def _ssd_kernel(z_ref, xbc_ref, dt_ref, cw_ref, cb_ref, hp_ref, cp_ref, e_ref, y_ref, tail_ref, st_ref, *, pad):
    ck = CHUNK
    gp = B_HEADS_PER_GROUP * B_HEAD
    nsub = xbc_ref.shape[1] // ck
    step = pl.program_id(1)

    @pl.when(step == 0)
    def _():
        tail_ref[...] = jnp.zeros(tail_ref.shape, F32)
        st_ref[...] = jnp.zeros(st_ref.shape, F32)

    row = lax.broadcasted_iota(jnp.int32, (ck, 1), 0)
    ri = lax.broadcasted_iota(jnp.int32, (ck, ck), 0)
    ci = lax.broadcasted_iota(jnp.int32, (ck, ck), 1)
    causal = ri >= ci
    tril = jnp.where(causal, 1.0, 0.0).astype(BF16)
    triu = jnp.where(ri <= ci, 1.0, 0.0).astype(BF16)
    cw = cw_ref[...]
    hp = hp_ref[...]
    cp = cp_ref[...]
    d_skip, norm_g = cp[0:1], cp[1:2]

    def chunk(i, carry):
        rows = pl.ds(pl.multiple_of(i * ck, ck), ck)
        x_raw = xbc_ref[0, rows, :].astype(F32)
        tail = tail_ref[...]
        conv = cb_ref[...] + x_raw * cw[B_CONV - 1:B_CONV]
        for s in range(1, B_CONV):
            conv = conv + _shift_rows(x_raw, tail, s) * cw[B_CONV - 1 - s:B_CONV - s]
        tail_ref[...] = x_raw[ck - SUBLANES:]
        xa = _silu(conv)
        valid = ((step * nsub + i) * ck + row) >= pad
        xs = jnp.where(valid, xa[:, :B_WIDTH], 0.0)
        bm = xa[:, B_WIDTH:B_WIDTH + B_GROUPS * B_STATE]
        cm = xa[:, B_WIDTH + B_GROUPS * B_STATE:]

        dt = _softplus(dt_ref[0, rows, :].astype(F32) + hp[0:1])
        a = dt * (-jnp.exp(hp[1:2]))
        a_cs = _mm_exact_lhs(tril, a)
        a_cs_t = _mm_exact_rhs(a, triu, TN)
        e = e_ref[...]
        dt_e = _mm_exact_rhs(dt, e)
        acs_e = _mm_exact_rhs(a_cs, e)
        atot_e = acs_e[ck - 1:ck, :]
        xdt = xs * dt_e
        xw = xdt * jnp.exp(atot_e - acs_e)
        e_in = jnp.exp(acs_e)
        e_tot = jnp.exp(atot_e)

        z = z_ref[0, rows, :].astype(F32)
        outs = []
        for g in range(B_GROUPS):
            gs = slice(g * gp, (g + 1) * gp)
            bg = bm[:, g * B_STATE:(g + 1) * B_STATE]
            cg = cm[:, g * B_STATE:(g + 1) * B_STATE]
            cb = _mm(cg, bg, NT)
            st = st_ref[:, gs]
            y_off = _mm(cg, st) * e_in[:, gs]
            y_diag = []
            for r in range(B_HEADS_PER_GROUP):
                h = g * B_HEADS_PER_GROUP + r
                hs = slice(h * B_HEAD, (h + 1) * B_HEAD)
                seg = acs_e[:, hs] - a_cs_t[h:h + 1, :]
                lmat = jnp.exp(jnp.where(causal, seg, -1e30))
                y_diag.append(_mm(cb * lmat, xdt[:, hs]))
            st_ref[:, gs] = st * e_tot[:, gs] + _mm(bg, xw[:, gs], TN)
            y = jnp.concatenate(y_diag, axis=-1) + y_off
            y = (y + xs[:, gs] * d_skip[:, gs]) * _silu(z[:, gs])
            y = y * lax.rsqrt(jnp.mean(y * y, axis=-1, keepdims=True) + EPS)
            outs.append(y * norm_g[:, gs])
        y_ref[0, rows, :] = jnp.concatenate(outs, axis=-1).astype(y_ref.dtype)
        return carry

    lax.fori_loop(0, nsub, chunk, 0)


def _ssd_mix(proj, conv_w, conv_b, head_params, chan_params, expand, *, pad):
    bsz, lp, _ = proj.shape
    tr = _rows_per_step(lp)

    def col(off, width):
        return pl.BlockSpec((1, tr, width), lambda b, s, o=off // width: (b, s, o))

    def full(shape):
        return pl.BlockSpec(shape, lambda b, s: (0,) * len(shape))

    return pl.pallas_call(
        functools.partial(_ssd_kernel, pad=pad),
        grid=(bsz, lp // tr),
        in_specs=[col(COL_Z, B_WIDTH), col(COL_XBC, B_CONV_DIM), col(COL_DT, LANES),
                  full((8, B_CONV_DIM)), full((1, B_CONV_DIM)), full((8, LANES)), full((8, B_WIDTH)),
                  full((LANES, B_WIDTH))],
        out_specs=pl.BlockSpec((1, tr, B_WIDTH), lambda b, s: (b, s, 0)),
        out_shape=jax.ShapeDtypeStruct((bsz, lp, B_WIDTH), BF16),
        scratch_shapes=[pltpu.VMEM((SUBLANES, B_CONV_DIM), F32), pltpu.VMEM((B_STATE, B_WIDTH), F32)],
        compiler_params=pltpu.CompilerParams(dimension_semantics=("parallel", "arbitrary"),
                                             vmem_limit_bytes=VMEM_LIMIT),
        name="ssd",
    )(proj, proj, proj, conv_w, conv_b, head_params, chan_params, expand)


def _retention_kernel(q_ref, k_ref, v_ref, g_ref, cos_ref, sin_ref, dec_ref, y_ref, st_ref):
    ck = CHUNK
    nch = q_ref.shape[1] // ck

    @pl.when(pl.program_id(1) == 0)
    def _():
        st_ref[...] = jnp.zeros(st_ref.shape, F32)

    even = (lax.broadcasted_iota(jnp.int32, (1, C_QK), 1) % 2) == 0
    cos_t = cos_ref[...]
    sin_t = sin_ref[...]

    def rot(t):
        partner = jnp.where(even, pltpu.roll(t, C_QK - 1, 1), pltpu.roll(t, 1, 1))
        return t * cos_t + partner * sin_t

    q = rot(q_ref[0].astype(F32))
    k = rot(k_ref[0].astype(F32)) * (C_QK_HEAD ** -0.5)
    v = v_ref[0]
    gate = _silu(g_ref[0].astype(F32))

    pairs = [(c, h) for c in range(nch) for h in range(C_HEADS)]
    rows = [slice(c * ck, (c + 1) * ck) for c in range(nch)]
    qh = {(c, h): q[rows[c], h * C_QK_HEAD:(h + 1) * C_QK_HEAD] for c, h in pairs}
    kh = {(c, h): k[rows[c], h * C_QK_HEAD:(h + 1) * C_QK_HEAD] for c, h in pairs}
    vh = {(c, h): v[rows[c], h * C_V_HEAD:(h + 1) * C_V_HEAD].astype(BF16) for c, h in pairs}
    scores = {p: _mm(qh[p], kh[p], NT) * dec_ref[3 * p[1]] for p in pairs}
    y_intra = {p: _dg(scores[p].astype(BF16), vh[p], NN) for p in pairs}
    kv = {p: _dg((kh[p] * dec_ref[3 * p[1] + 2]).astype(BF16), vh[p], TN) for p in pairs}
    s_in = {}
    for h in range(C_HEADS):
        chunk_decay = (1.0 - 2.0 ** (-5.0 - h)) ** ck
        st = st_ref[h]
        for c in range(nch):
            s_in[(c, h)] = st
            st = st * chunk_decay + kv[(c, h)]
        st_ref[h] = st
    y = {p: y_intra[p] + _mm(qh[p] * dec_ref[3 * p[1] + 1], s_in[p]) for p in pairs}
    yc = {p: y[p] - jnp.mean(y[p], axis=-1, keepdims=True) for p in pairs}
    yn = {p: yc[p] * lax.rsqrt(jnp.mean(yc[p] * yc[p], axis=-1, keepdims=True) + EPS) for p in pairs}
    out = jnp.concatenate([jnp.concatenate([yn[(c, h)] for h in range(C_HEADS)], axis=-1)
                           for c in range(nch)], axis=0)
    y_ref[0] = (gate * out).astype(y_ref.dtype)


def _retention_decay_tables():
    idx = np.arange(CHUNK, dtype=np.float64)
    rel = idx[:, None] - idx[None, :]
    tabs = []
    for h in range(C_HEADS):
        log_gamma = math.log(1.0 - 2.0 ** (-5.0 - h))
        tabs.append(np.where(rel >= 0, np.exp(np.maximum(rel, 0.0) * log_gamma), 0.0))
        tabs.append(np.broadcast_to(np.exp((idx + 1.0) * log_gamma)[:, None], (CHUNK, C_QK_HEAD)))
        tabs.append(np.broadcast_to(np.exp((CHUNK - 1.0 - idx) * log_gamma)[:, None], (CHUNK, C_QK_HEAD)))
    return jnp.asarray(np.stack(tabs), F32)


def _retention_mix(proj, cos_t, sin_t):
    bsz, lp, _ = proj.shape
    tr = _rows_per_step(lp)

    def col(off, width):
        return pl.BlockSpec((1, tr, width), lambda b, s, o=off // width: (b, s, o))

    tab = pl.BlockSpec((tr, C_QK), lambda b, s: (s, 0))
    return pl.pallas_call(
        _retention_kernel,
        grid=(bsz, lp // tr),
        in_specs=[col(COL_RET_Q, C_QK), col(COL_RET_K, C_QK), col(COL_RET_V, C_WIDTH),
                  col(COL_RET_G, C_WIDTH), tab, tab,
                  pl.BlockSpec((3 * C_HEADS, CHUNK, CHUNK), lambda b, s: (0, 0, 0))],
        out_specs=pl.BlockSpec((1, tr, C_WIDTH), lambda b, s: (b, s, 0)),
        out_shape=jax.ShapeDtypeStruct((bsz, lp, C_WIDTH), BF16),
        scratch_shapes=[pltpu.VMEM((C_HEADS, C_QK_HEAD, C_V_HEAD), F32)],
        compiler_params=pltpu.CompilerParams(dimension_semantics=("parallel", "arbitrary"),
                                             vmem_limit_bytes=VMEM_LIMIT),
        name="retention",
    )(proj, proj, proj, proj, cos_t, sin_t, _retention_decay_tables())


def _merge_kernel(ya_ref, yb_ref, yc_ref, g0_ref, g1_ref, g2_ref, gb_ref, h_ref,
                  wa_ref, wb0_ref, wb1_ref, wc_ref, wo_ref, o_ref):
    gb = gb_ref[...]
    d = D_MODEL
    yb = yb_ref[...]
    branch_b = _dg(yb[:, :d], wb0_ref[...], NN) + _dg(yb[:, d:], wb1_ref[...], NN)
    merged = (_sigmoid(g0_ref[...].astype(F32) + gb[:, :d]) * _dg(ya_ref[...], wa_ref[...], NN)
              + _sigmoid(g1_ref[...].astype(F32) + gb[:, d:2 * d]) * branch_b
              + _sigmoid(g2_ref[...].astype(F32) + gb[:, 2 * d:]) * _dg(yc_ref[...], wc_ref[...], NN))
    o_ref[...] = h_ref[...] + _dg(merged.astype(BF16), wo_ref[...], NN)


def _merge(ya, yb, yc, proj2d, gate_bias, h2d, w_branch, wo):
    m, d = h2d.shape
    assert A_WIDTH == d and B_WIDTH == 2 * d and C_WIDTH == d
    tm = _largest_divisor(m, 352)

    def rows(width):
        return pl.BlockSpec((tm, width), lambda i: (i, 0))

    def gcol(j):
        return pl.BlockSpec((tm, d), lambda i, o=COL_GATE // d + j: (i, o))

    def full(shape):
        return pl.BlockSpec(shape, lambda i: (0, 0))

    def wblock(k):
        return pl.BlockSpec((d, d), lambda i, k=k: (k, 0))

    return pl.pallas_call(
        _merge_kernel,
        grid=(m // tm,),
        in_specs=[rows(A_WIDTH), rows(B_WIDTH), rows(C_WIDTH), gcol(0), gcol(1), gcol(2),
                  full((1, GATE_IN)), rows(d),
                  wblock(0), wblock(1), wblock(2), wblock(3), full((d, d))],
        out_specs=rows(d),
        out_shape=jax.ShapeDtypeStruct((m, d), F32),
        compiler_params=pltpu.CompilerParams(dimension_semantics=("parallel",),
                                             vmem_limit_bytes=VMEM_LIMIT),
        name="merge",
    )(ya, yb, yc, proj2d, proj2d, proj2d, gate_bias, h2d, w_branch, w_branch, w_branch, w_branch, wo)


def _matmul_residual_kernel(a_ref, w_ref, h_ref, o_ref):
    o_ref[...] = h_ref[...] + _dg(a_ref[...], w_ref[...], NN)


def _matmul_residual(a, w, h2d):
    m, d = h2d.shape
    kdim = a.shape[1]
    tm = _largest_divisor(m, 1056)
    return pl.pallas_call(
        _matmul_residual_kernel,
        grid=(m // tm,),
        in_specs=[pl.BlockSpec((tm, kdim), lambda i: (i, 0)), pl.BlockSpec((kdim, d), lambda i: (0, 0)),
                  pl.BlockSpec((tm, d), lambda i: (i, 0))],
        out_specs=pl.BlockSpec((tm, d), lambda i: (i, 0)),
        out_shape=jax.ShapeDtypeStruct((m, d), F32),
        compiler_params=pltpu.CompilerParams(dimension_semantics=("parallel",),
                                             vmem_limit_bytes=VMEM_LIMIT),
        name="ffn_down",
    )(a, w, h2d)


def _final_norm_kernel(h_ref, g_ref, o_ref, *, skip):
    x = h_ref[0, skip:, :]
    o_ref[0] = x * lax.rsqrt(jnp.mean(x * x, axis=-1, keepdims=True) + EPS) * g_ref[...]


def _final_norm(h3d, g, *, skip, seq):
    bsz, lp, d = h3d.shape
    assert lp == skip + seq
    return pl.pallas_call(
        functools.partial(_final_norm_kernel, skip=skip),
        grid=(bsz,),
        in_specs=[pl.BlockSpec((1, lp, d), lambda b: (b, 0, 0)), pl.BlockSpec((1, d), lambda b: (0, 0))],
        out_specs=pl.BlockSpec((1, seq, d), lambda b: (b, 0, 0)),
        out_shape=jax.ShapeDtypeStruct((bsz, seq, d), F32),
        compiler_params=pltpu.CompilerParams(dimension_semantics=("parallel",),
                                             vmem_limit_bytes=VMEM_LIMIT),
        name="final_norm",
    )(h3d, g.reshape(1, d))


def _pad_cols(t, width):
    return jnp.pad(t, [(0, 0)] * (t.ndim - 1) + [(0, width - t.shape[-1])])


def _pad_rows(t, rows):
    return jnp.pad(t, [(0, rows - t.shape[0])] + [(0, 0)] * (t.ndim - 1))


def _pack_in_proj(w, w_vres):
    a, b, c, g = (w[:, :A_IN], w[:, A_IN:A_IN + B_IN], w[:, A_IN + B_IN:A_IN + B_IN + C_IN],
                  w[:, A_IN + B_IN + C_IN:])
    aw = 3 * A_WIDTH
    parts = [b[:, B_WIDTH:B_WIDTH + B_CONV_DIM], g, b[:, :B_WIDTH],
             c[:, 2 * C_QK:2 * C_QK + C_WIDTH], c[:, 2 * C_QK + C_WIDTH:], c[:, :C_QK], c[:, C_QK:2 * C_QK],
             a[:, :aw], a[:, aw:aw + LANES], a[:, aw + LANES:],
             _pad_cols(w_vres, LANES), _pad_cols(b[:, B_WIDTH + B_CONV_DIM:], LANES)]
    out = jnp.concatenate(parts, axis=1)
    assert out.shape[1] == N_PROJ
    return out.astype(BF16)


def kernel(x, meta, norm_mix, norm_ffn, norm_final, w_in, w_in_vres, gate_bias, rwkv_mu, rwkv_mu_vres, rwkv_w0, rwkv_w2, rwkv_a0, rwkv_a2, rwkv_v0, rwkv_v2, rwkv_g2, rwkv_k_k, rwkv_k_a, rwkv_r_k, rwkv_ln_g, rwkv_ln_b, ssm_conv_w, ssm_conv_b, ssm_dt_bias, ssm_a_log, ssm_d, ssm_norm_g, w_branch, w_out, ffn_w_up, ffn_conv_w, ffn_conv_b, ffn_w_down):
    bsz, seq, d = x.shape
    depth = w_in.shape[0]
    length = N_META + seq
    pad = (-length) % CHUNK
    lp = pad + length
    assert (pad + N_META) % CHUNK == 0 and seq % CHUNK == 0
    m = bsz * lp

    h = jnp.concatenate([jnp.zeros((bsz, pad, d), F32),
                         jnp.broadcast_to(meta.astype(F32), (bsz, N_META, d)), x.astype(F32)], axis=1)
    h = h.reshape(m, d)

    pos = jnp.arange(length, dtype=F32)
    inv_freq = 1.0 / (ROPE_BASE ** jnp.linspace(0.0, 1.0, C_QK_HEAD // 2, dtype=F32))
    ang = pos[:, None] * inv_freq[None, :]
    cos_t = jnp.tile(jnp.repeat(jnp.cos(ang), 2, axis=1), (1, C_HEADS))
    sin_t = jnp.tile((jnp.repeat(jnp.sin(ang), 2, axis=1)
                      * jnp.tile(jnp.array([-1.0, 1.0], F32), C_QK_HEAD // 2)), (1, C_HEADS))
    cos_t = jnp.pad(cos_t, ((pad, 0), (0, 0)))
    sin_t = jnp.pad(sin_t, ((pad, 0), (0, 0)))

    expand = (jnp.arange(LANES)[:, None] == (jnp.arange(B_WIDTH)[None, :] // B_HEAD)).astype(BF16)

    heads = RWKV_HEADS_PER_STEP
    gw = heads * A_HEAD
    ngrp = A_WIDTH // gw
    aw = 3 * A_WIDTH
    v_first = None
    for l in range(depth):
        has_vres = l > 0
        w_vres = w_in_vres[l - 1] if has_vres else jnp.zeros((d, A_VRES_LORA), F32)
        proj = _norm_matmul(h, norm_mix[l], _pack_in_proj(w_in[l], w_vres), lp=lp, pad=pad, tn=512,
                            out_dtype=BF16)
        proj3 = proj.reshape(bsz, lp, N_PROJ)

        mu = rwkv_mu[l]
        zeros_w = jnp.zeros((A_WIDTH,), F32)
        pvec = jnp.stack([mu[:A_WIDTH], mu[A_WIDTH:2 * A_WIDTH], mu[2 * A_WIDTH:aw],
                          rwkv_w0[l], rwkv_a0[l], rwkv_k_k[l], rwkv_k_a[l], rwkv_r_k[l].reshape(-1),
                          rwkv_ln_g[l], rwkv_ln_b[l], rwkv_v0[l - 1] if has_vres else zeros_w]
                         + [zeros_w] * 5)
        mu_vr = _pad_cols(rwkv_mu_vres[l - 1], LANES) if has_vres else jnp.zeros((LANES,), F32)
        lmu = _pad_rows(jnp.stack([mu[aw:aw + LANES], mu[aw + LANES:], mu_vr]), 8)
        w2g = rwkv_w2[l].reshape(A_DECAY_LORA, ngrp, gw).transpose(1, 0, 2)
        a2g = rwkv_a2[l].reshape(A_ICLR_LORA, ngrp, gw).transpose(1, 0, 2)
        wwa = jnp.concatenate([jnp.concatenate([w2g, jnp.zeros_like(w2g)], axis=2),
                               jnp.concatenate([jnp.zeros_like(a2g), a2g], axis=2)], axis=1).astype(BF16)
        v2 = _pad_rows(rwkv_v2[l - 1], LANES).astype(BF16) if has_vres else None
        y_a, v_first = _rwkv_mix(proj3, v_first, pvec, lmu, wwa, rwkv_g2[l].astype(BF16), v2,
                                 has_vres=has_vres)

        head_params = _pad_rows(jnp.stack([_pad_cols(ssm_dt_bias[l], LANES), _pad_cols(ssm_a_log[l], LANES)]), 8)
        chan_params = _pad_rows(jnp.stack([jnp.repeat(ssm_d[l], B_HEAD), ssm_norm_g[l]]), 8)
        y_b = _ssd_mix(proj3, _pad_rows(ssm_conv_w[l], 8), ssm_conv_b[l].reshape(1, -1), head_params,
                       chan_params, expand, pad=pad)

        y_c = _retention_mix(proj3, cos_t, sin_t)

        h = _merge(y_a.reshape(m, A_WIDTH), y_b.reshape(m, B_WIDTH), y_c.reshape(m, C_WIDTH), proj,
                   gate_bias[l].reshape(1, -1), h, w_branch[l].astype(BF16), w_out[l].astype(BF16))

        act = _ffn_up(h, norm_ffn[l], ffn_w_up[l].astype(BF16), _pad_rows(ffn_conv_w[l], 8),
                      ffn_conv_b[l].reshape(1, -1), lp=lp, pad=pad)
        h = _matmul_residual(act, ffn_w_down[l].astype(BF16), h)

    return _final_norm(h.reshape(bsz, lp, d), norm_final, skip=pad + N_META, seq=seq)
```

```python
import functools
import math

import numpy as np
import jax
import jax.numpy as jnp
from jax import lax
from jax.experimental import pallas as pl
from jax.experimental.pallas import tpu as pltpu

F32 = jnp.float32
BF16 = jnp.bfloat16

D_MODEL = 1024
CHUNK = 64
N_META = 16
EPS = 1e-6

A_HEAD = 64
A_WIDTH = D_MODEL
A_HEADS = A_WIDTH // A_HEAD
A_DECAY_LORA = 64
A_ICLR_LORA = 64
A_VRES_LORA = 32
A_GATE_LORA = 128
A_LN_EPS = 64e-5
A_IN = 3 * A_WIDTH + A_DECAY_LORA + A_ICLR_LORA + A_GATE_LORA

B_WIDTH = 2 * D_MODEL
B_HEAD = 64
B_HEADS = B_WIDTH // B_HEAD
B_GROUPS = 4
B_HEADS_PER_GROUP = B_HEADS // B_GROUPS
B_STATE = 128
B_CONV = 4
B_CONV_DIM = B_WIDTH + 2 * B_GROUPS * B_STATE
B_IN = B_WIDTH + B_CONV_DIM + B_HEADS

C_HEADS = 8
C_QK_HEAD = 64
C_V_HEAD = 128
C_QK = C_HEADS * C_QK_HEAD
C_WIDTH = C_HEADS * C_V_HEAD
C_IN = 2 * C_QK + 2 * C_WIDTH
ROPE_BASE = 10000.0

GATE_IN = 3 * D_MODEL
FFN_HIDDEN = 2816
FFN_CONV = 3
FFN_ROW_BLOCKS = 12
FFN_TILE = 256

LANES = 128
SUBLANES = 8

COL_XBC = 0
COL_GATE = 3072
COL_Z = 6144
COL_RET_V = 8192
COL_RET_G = 9216
COL_RET_Q = 10240
COL_RET_K = 10752
COL_A_R = 11264
COL_A_K = 12288
COL_A_V = 13312
COL_A_WA = 14336
COL_A_GL = 14464
COL_A_VR = 14592
COL_DT = 14720
N_PROJ = 14848

RWKV_HEADS_PER_STEP = 16
CHUNKS_PER_STEP = 11
VMEM_LIMIT = 48 * 1024 * 1024
NORM_MATMUL_ROWS = 2112

NN = (((1,), (0,)), ((), ()))
NT = (((1,), (1,)), ((), ()))
TN = (((0,), (0,)), ((), ()))


def _dg(a, b, dims):
    return lax.dot_general(a, b, dims, preferred_element_type=F32)


def _split3(x):
    hi = x.astype(BF16)
    r = x - hi.astype(F32)
    mid = r.astype(BF16)
    lo = (r - mid.astype(F32)).astype(BF16)
    return hi, mid, lo


def _mm(a, b, dims=NN):
    return _dg(a.astype(BF16), b.astype(BF16), dims)


def _mm_exact_lhs(a_bf16, b, dims=NN):
    h, m, l = _split3(b)
    return _dg(a_bf16, h, dims) + (_dg(a_bf16, m, dims) + _dg(a_bf16, l, dims))


def _mm_exact_rhs(a, b_bf16, dims=NN):
    h, m, l = _split3(a)
    return _dg(h, b_bf16, dims) + (_dg(m, b_bf16, dims) + _dg(l, b_bf16, dims))


def _sigmoid(x):
    return 1.0 / (1.0 + jnp.exp(-x))


def _silu(x):
    return x * _sigmoid(x)


def _softplus(x):
    return jnp.maximum(x, 0.0) + jnp.log(1.0 + jnp.exp(-jnp.abs(x)))


def _shift_rows(x, tail, s):
    rolled = pltpu.roll(x, s, 0)
    row = lax.broadcasted_iota(jnp.int32, (SUBLANES, 1), 0)
    top = jnp.where(row < s, pltpu.roll(tail, s, 0), rolled[:SUBLANES])
    return jnp.concatenate([top, rolled[SUBLANES:]], axis=0)


def _rows_per_step(lp):
    return CHUNK * _largest_divisor(lp // CHUNK, CHUNKS_PER_STEP, 1)


def _largest_divisor(n, cap, mult=8):
    best = None
    for d in range(mult, cap + 1, mult):
        if n % d == 0:
            best = d
    assert best is not None
    return best


def _norm_matmul_kernel(h_ref, g_ref, w_ref, o_ref, u_ref, *, tm, lp, pad):
    i = pl.program_id(0)

    @pl.when(pl.program_id(1) == 0)
    def _():
        x = h_ref[...]
        y = x * lax.rsqrt(jnp.mean(x * x, axis=-1, keepdims=True) + EPS) * g_ref[...]
        pos = (i % (lp // tm)) * tm + lax.broadcasted_iota(jnp.int32, (tm, 1), 0)
        u_ref[...] = jnp.where(pos >= pad, y, 0.0).astype(BF16)

    o_ref[...] = jnp.dot(u_ref[...], w_ref[...], preferred_element_type=F32).astype(o_ref.dtype)


def _norm_matmul(h2d, g, w, *, lp, pad, tn, out_dtype):
    m, d = h2d.shape
    n = w.shape[1]
    tm = _largest_divisor(lp, NORM_MATMUL_ROWS)
    assert m % tm == 0 and n % tn == 0
    return pl.pallas_call(
        functools.partial(_norm_matmul_kernel, tm=tm, lp=lp, pad=pad),
        grid=(m // tm, n // tn),
        in_specs=[pl.BlockSpec((tm, d), lambda i, j: (i, 0)),
                  pl.BlockSpec((1, d), lambda i, j: (0, 0)),
                  pl.BlockSpec((d, tn), lambda i, j: (0, j))],
        out_specs=pl.BlockSpec((tm, tn), lambda i, j: (i, j)),
        out_shape=jax.ShapeDtypeStruct((m, n), out_dtype),
        scratch_shapes=[pltpu.VMEM((tm, d), BF16)],
        compiler_params=pltpu.CompilerParams(dimension_semantics=("parallel", "arbitrary"),
                                             vmem_limit_bytes=VMEM_LIMIT),
        name="norm_matmul",
    )(h2d, g.reshape(1, d), w)


def _ffn_up_kernel(h_ref, g_ref, wg_ref, wv_ref, cwg_ref, cwv_ref, cbg_ref, cbv_ref, o_ref, u_ref, *, pad):
    @pl.when(pl.program_id(1) == 0)
    def _():
        x = h_ref[...]
        y = x * lax.rsqrt(jnp.mean(x * x, axis=-1, keepdims=True) + EPS) * g_ref[...]
        pos = lax.broadcasted_iota(jnp.int32, (x.shape[0], 1), 0)
        u_ref[...] = jnp.where(pos >= pad, y, 0.0).astype(BF16)

    rows = u_ref.shape[0]
    rb = _largest_divisor(rows, rows // FFN_ROW_BLOCKS, 16)

    def conv(up, tail, cw_ref, cb_ref):
        cw = cw_ref[...]
        out = cb_ref[...] + up * cw[FFN_CONV - 1:FFN_CONV]
        for s in range(1, FFN_CONV):
            out = out + _shift_rows(up, tail, s) * cw[FFN_CONV - 1 - s:FFN_CONV - s]
        return out

    tail_g = tail_v = jnp.zeros((SUBLANES, FFN_TILE), F32)
    for r0 in range(0, rows, rb):
        u = u_ref[r0:r0 + rb, :]
        up_g = jnp.dot(u, wg_ref[...], preferred_element_type=F32)
        up_v = jnp.dot(u, wv_ref[...], preferred_element_type=F32)
        gate = conv(up_g, tail_g, cwg_ref, cbg_ref)
        val = conv(up_v, tail_v, cwv_ref, cbv_ref)
        tail_g, tail_v = up_g[rb - SUBLANES:], up_v[rb - SUBLANES:]
        o_ref[r0:r0 + rb, :] = (_silu(gate) * val).astype(o_ref.dtype)


def _ffn_up(h2d, g, w, conv_w, conv_b, *, lp, pad):
    m, d = h2d.shape
    assert m % lp == 0
    nt = FFN_HIDDEN // FFN_TILE

    def cols(rows, off):
        return pl.BlockSpec((rows, FFN_TILE), lambda i, j, o=off: (0, j + o))

    return pl.pallas_call(
        functools.partial(_ffn_up_kernel, pad=pad),
        grid=(m // lp, nt),
        in_specs=[pl.BlockSpec((lp, d), lambda i, j: (i, 0)),
                  pl.BlockSpec((1, d), lambda i, j: (0, 0)),
                  cols(d, 0), cols(d, nt), cols(8, 0), cols(8, nt), cols(1, 0), cols(1, nt)],
        out_specs=pl.BlockSpec((lp, FFN_TILE), lambda i, j: (i, j)),
        out_shape=jax.ShapeDtypeStruct((m, FFN_HIDDEN), BF16),
        scratch_shapes=[pltpu.VMEM((lp, d), BF16)],
        compiler_params=pltpu.CompilerParams(dimension_semantics=("parallel", "arbitrary"),
                                             vmem_limit_bytes=VMEM_LIMIT),
        name="ffn_up_conv_gate",
    )(h2d, g.reshape(1, d), w, w, conv_w, conv_w, conv_b, conv_b)


def _unit_lower_inverse(ns, eye, same16, same32):
    ps = [jnp.where(same16, n, 0.0) for n in ns]
    ts = [eye + p for p in ps]
    for _ in range(3):
        ps = [_mm(p, p) for p in ps]
        ts = [t + _mm(t, p) for t, p in zip(ts, ps)]
    offs = [jnp.where(same32, n, 0.0) - jnp.where(same16, n, 0.0) for n in ns]
    tmp = [_mm(t, o) for t, o in zip(ts, offs)]
    ts = [t + _mm(x, t) for t, x in zip(ts, tmp)]
    offs = [jnp.where(same32, 0.0, n) for n in ns]
    tmp = [_mm(t, o) for t, o in zip(ts, offs)]
    return [t + _mm(x, t) for t, x in zip(ts, tmp)]


def _rwkv_kernel(*refs, heads, has_vres):
    it = iter(refs)
    pr_ref, pk_ref, pv_ref, pwa_ref, pgl_ref = (next(it) for _ in range(5))
    if has_vres:
        pvr_ref, vf_ref = next(it), next(it)
    pvec_ref, lmu_ref, wwa_ref, g2_ref = (next(it) for _ in range(4))
    if has_vres:
        v2_ref = next(it)
    y_ref = next(it)
    if not has_vres:
        vfo_ref = next(it)
    prev_r, prev_k, prev_v, prev_wa, prev_gl = (next(it) for _ in range(5))
    if has_vres:
        prev_vr = next(it)
    st_ref = next(it)

    ck = CHUNK
    gw = heads * A_HEAD
    tails = (prev_r, prev_k, prev_v, prev_wa, prev_gl) + ((prev_vr,) if has_vres else ())

    @pl.when(pl.program_id(2) == 0)
    def _():
        for ref in tails:
            ref[...] = jnp.zeros(ref.shape, F32)
        st_ref[...] = jnp.zeros(st_ref.shape, F32)

    pvec = pvec_ref[...]
    mu_r, mu_k, mu_v, w0, a0, k_k, k_a, r_k, ln_g, ln_b, v0 = (pvec[i:i + 1] for i in range(11))
    lmu = lmu_ref[...]
    lane = lax.broadcasted_iota(jnp.int32, (1, LANES), 1)
    ri = lax.broadcasted_iota(jnp.int32, (ck, ck), 0)
    ci = lax.broadcasted_iota(jnp.int32, (ck, ck), 1)
    strict = ri > ci
    same16 = (ri // 16) == (ci // 16)
    same32 = (ri // 32) == (ci // 32)
    eye = jnp.where(ri == ci, 1.0, 0.0).astype(F32)
    tril = jnp.where(ri >= ci, 1.0, 0.0).astype(BF16)
    incl = ri >= ci
    hsl = [slice(h * A_HEAD, (h + 1) * A_HEAD) for h in range(heads)]

    def chunk(i, carry):
        rows = pl.ds(pl.multiple_of(i * ck, ck), ck)

        def tshift(x_ref, tail_ref, mu):
            x = x_ref[0, rows, :].astype(F32)
            prev = _shift_rows(x, tail_ref[...], 1)
            tail_ref[...] = x[ck - SUBLANES:]
            return x + (prev - x) * mu

        r = tshift(pr_ref, prev_r, mu_r)
        k = tshift(pk_ref, prev_k, mu_k)
        v = tshift(pv_ref, prev_v, mu_v)
        wa = tshift(pwa_ref, prev_wa, lmu[0:1])
        gl = tshift(pgl_ref, prev_gl, lmu[1:2])

        wa_act = jnp.where(lane < A_DECAY_LORA, jnp.tanh(wa), wa)
        wa_out = _mm(wa_act, wwa_ref[0])
        w_log = -_softplus(-(w0 + wa_out[:, :gw])) - 0.5
        lw = -jnp.exp(w_log)
        a = _sigmoid(a0 + wa_out[:, gw:])
        g = _mm(_sigmoid(gl), g2_ref[...])
        if has_vres:
            vr = tshift(pvr_ref, prev_vr, lmu[2:3])
            v = v + (vf_ref[0, rows, :] - v) * _sigmoid(v0 + _mm(vr, v2_ref[...]))
        else:
            vfo_ref[0, rows, :] = v

        cum = _mm_exact_lhs(tril, lw)
        cum_last = cum[ck - 1:ck, :]
        e_neg = jnp.exp(-cum)
        e_end = jnp.exp(cum_last - cum)
        w_end = jnp.exp(cum_last)

        kk_raw = k * k_k
        kka = kk_raw * a
        kh = k * (1.0 + (a - 1.0) * k_a)
        rt_all = r * jnp.exp(cum)
        at_all = -(kk_raw * jnp.exp(cum - lw))
        khat_all = kh * e_neg
        bhat_all = kka * e_neg
        kt_all = kh * e_end
        bt_all = kka * e_end
        rk_all = r * kh * r_k

        def head_group(hids):
            hsl_ = [hsl[h] for h in hids]
            nrm = [lax.rsqrt(jnp.maximum(jnp.sum(kk_raw[:, hs] * kk_raw[:, hs], axis=-1, keepdims=True), 1e-24))
                   for hs in hsl_]
            vh = [v[:, hs] for hs in hsl_]
            s0 = [st_ref[h] for h in hids]
            at = [at_all[:, hs] * n for hs, n in zip(hsl_, nrm)]
            bhat = [bhat_all[:, hs] * n for hs, n in zip(hsl_, nrm)]
            bt = [bt_all[:, hs] * n for hs, n in zip(hsl_, nrm)]
            ar = [jnp.concatenate([x, rt_all[:, hs]], axis=0) for x, hs in zip(at, hsl_)]
            mk = [_mm(x, khat_all[:, hs], NT) for x, hs in zip(ar, hsl_)]
            mb = [_mm(x, y, NT) for x, y in zip(ar, bhat)]
            a_ak = [jnp.where(strict, x[:ck], 0.0) for x in mk]
            a_rk = [jnp.where(incl, x[ck:], 0.0) for x in mk]
            a_ab = [jnp.where(strict, x[:ck], 0.0) for x in mb]
            a_rb = [jnp.where(incl, x[ck:], 0.0) for x in mb]
            tinv = _unit_lower_inverse(a_ab, eye, same16, same32)
            ars = [_mm(x, s, NT) for x, s in zip(ar, s0)]
            rhs = [x[:ck] + _mm(m_, v_) for x, m_, v_ in zip(ars, a_ak, vh)]
            u = [_mm(t, x) for t, x in zip(tinv, rhs)]
            ys = [x[ck:] + _mm(m_, v_) + _mm(n_, u_) for x, m_, v_, n_, u_ in zip(ars, a_rk, vh, a_rb, u)]
            res = []
            for n_, h in enumerate(hids):
                hs = hsl[h]
                st_ref[h] = s0[n_] * w_end[:, hs] + _mm(vh[n_], kt_all[:, hs], TN) + _mm(u[n_], bt[n_], TN)
                yc = ys[n_] - jnp.mean(ys[n_], axis=-1, keepdims=True)
                yn = yc * lax.rsqrt(jnp.mean(yc * yc, axis=-1, keepdims=True) + A_LN_EPS)
                bonus = jnp.sum(rk_all[:, hs], axis=-1, keepdims=True) * vh[n_]
                res.append((yn * ln_g[:, hs] + ln_b[:, hs] + bonus) * g[:, hs])
            return res

        outs = head_group(list(range(heads)))
        y_ref[0, rows, :] = jnp.concatenate(outs, axis=-1).astype(y_ref.dtype)
        return carry

    lax.fori_loop(0, pr_ref.shape[1] // ck, chunk, 0)


def _rwkv_mix(proj, v_first, pvec, lmu, wwa, g2, v2, *, has_vres):
    bsz, lp, _ = proj.shape
    heads = RWKV_HEADS_PER_STEP
    gw = heads * A_HEAD
    ngrp = A_WIDTH // gw
    tr = _rows_per_step(lp)

    def col(off, width):
        return pl.BlockSpec((1, tr, width), lambda b, hg, s, o=off // width: (b, s, o))

    def colg(off):
        return pl.BlockSpec((1, tr, gw), lambda b, hg, s, o=off // gw: (b, s, o + hg))

    act_g = pl.BlockSpec((1, tr, gw), lambda b, hg, s: (b, s, hg))
    in_specs = [colg(COL_A_R), colg(COL_A_K), colg(COL_A_V), col(COL_A_WA, LANES), col(COL_A_GL, LANES)]
    args = [proj] * 5
    if has_vres:
        in_specs += [col(COL_A_VR, LANES), act_g]
        args += [proj, v_first]
    in_specs += [pl.BlockSpec((16, gw), lambda b, hg, s: (0, hg)),
                 pl.BlockSpec((8, LANES), lambda b, hg, s: (0, 0)),
                 pl.BlockSpec((1, LANES, 2 * gw), lambda b, hg, s: (hg, 0, 0)),
                 pl.BlockSpec((A_GATE_LORA, gw), lambda b, hg, s: (0, hg))]
    args += [pvec, lmu, wwa, g2]
    if has_vres:
        in_specs.append(pl.BlockSpec((LANES, gw), lambda b, hg, s: (0, hg)))
        args.append(v2)
    y_shape = jax.ShapeDtypeStruct((bsz, lp, A_WIDTH), BF16)
    if has_vres:
        out_shape, out_specs = y_shape, act_g
    else:
        out_shape = (y_shape, jax.ShapeDtypeStruct((bsz, lp, A_WIDTH), F32))
        out_specs = (act_g, act_g)
    scratch = ([pltpu.VMEM((SUBLANES, gw), F32)] * 3
               + [pltpu.VMEM((SUBLANES, LANES), F32)] * (3 if has_vres else 2))
    scratch.append(pltpu.VMEM((heads, A_HEAD, A_HEAD), F32))
    out = pl.pallas_call(
        functools.partial(_rwkv_kernel, heads=heads, has_vres=has_vres),
        grid=(bsz, ngrp, lp // tr),
        in_specs=in_specs, out_specs=out_specs, out_shape=out_shape,
        scratch_shapes=scratch,
        compiler_params=pltpu.CompilerParams(dimension_semantics=("parallel", "parallel", "arbitrary"),
                                             vmem_limit_bytes=VMEM_LIMIT),
        name="rwkv7_vres" if has_vres else "rwkv7",
    )(*args)
    return (out, v_first) if has_vres else out


def _ssd_kernel(z_ref, xbc_ref, dt_ref, cw_ref, cb_ref, hp_ref, cp_ref, e_ref, y_ref, tail_ref, st_ref, *, pad):
    ck = CHUNK
    gp = B_HEADS_PER_GROUP * B_HEAD
    nsub = xbc_ref.shape[1] // ck
    step = pl.program_id(1)

    @pl.when(step == 0)
    def _():
        tail_ref[...] = jnp.zeros(tail_ref.shape, F32)
        st_ref[...] = jnp.zeros(st_ref.shape, F32)

    row = lax.broadcasted_iota(jnp.int32, (ck, 1), 0)
    ri = lax.broadcasted_iota(jnp.int32, (ck, ck), 0)
    ci = lax.broadcasted_iota(jnp.int32, (ck, ck), 1)
    causal = ri >= ci
    tril = jnp.where(causal, 1.0, 0.0).astype(BF16)
    triu = jnp.where(ri <= ci, 1.0, 0.0).astype(BF16)
    cw = cw_ref[...]
    hp = hp_ref[...]
    cp = cp_ref[...]
    d_skip, norm_g = cp[0:1], cp[1:2]

    def chunk(i, carry):
        rows = pl.ds(pl.multiple_of(i * ck, ck), ck)
        x_raw = xbc_ref[0, rows, :].astype(F32)
        tail = tail_ref[...]
        conv = cb_ref[...] + x_raw * cw[B_CONV - 1:B_CONV]
        for s in range(1, B_CONV):
            conv = conv + _shift_rows(x_raw, tail, s) * cw[B_CONV - 1 - s:B_CONV - s]
        tail_ref[...] = x_raw[ck - SUBLANES:]
        xa = _silu(conv)
        valid = ((step * nsub + i) * ck + row) >= pad
        xs = jnp.where(valid, xa[:, :B_WIDTH], 0.0)
        bm = xa[:, B_WIDTH:B_WIDTH + B_GROUPS * B_STATE]
        cm = xa[:, B_WIDTH + B_GROUPS * B_STATE:]

        dt = _softplus(dt_ref[0, rows, :].astype(F32) + hp[0:1])
        a = dt * (-jnp.exp(hp[1:2]))
        a_cs = _mm_exact_lhs(tril, a)
        a_cs_t = _mm_exact_rhs(a, triu, TN)
        e = e_ref[...]
        dt_e = _mm_exact_rhs(dt, e)
        acs_e = _mm_exact_rhs(a_cs, e)
        atot_e = acs_e[ck - 1:ck, :]
        xdt = xs * dt_e
        xw = xdt * jnp.exp(atot_e - acs_e)
        e_in = jnp.exp(acs_e)
        e_tot = jnp.exp(atot_e)

        z = z_ref[0, rows, :].astype(F32)
        gr = range(B_GROUPS)
        gsl = [slice(g * gp, (g + 1) * gp) for g in gr]
        bg = [bm[:, g * B_STATE:(g + 1) * B_STATE] for g in gr]
        cg = [cm[:, g * B_STATE:(g + 1) * B_STATE] for g in gr]
        cb = [_mm(cg[g], bg[g], NT) for g in gr]
        st = [st_ref[:, gs] for gs in gsl]
        y_off = [_mm(cg[g], st[g]) * e_in[:, gsl[g]] for g in gr]
        hsl = [slice(h * B_HEAD, (h + 1) * B_HEAD) for h in range(B_HEADS)]
        lmat = [jnp.exp(jnp.where(causal, acs_e[:, hsl[h]] - a_cs_t[h:h + 1, :], -1e30)) for h in range(B_HEADS)]
        y_diag = [_mm(cb[h // B_HEADS_PER_GROUP] * lmat[h], xdt[:, hsl[h]]) for h in range(B_HEADS)]
        for g in gr:
            st_ref[:, gsl[g]] = st[g] * e_tot[:, gsl[g]] + _mm(bg[g], xw[:, gsl[g]], TN)
        outs = []
        for g in gr:
            gs = gsl[g]
            y = jnp.concatenate(y_diag[g * B_HEADS_PER_GROUP:(g + 1) * B_HEADS_PER_GROUP], axis=-1) + y_off[g]
            y = (y + xs[:, gs] * d_skip[:, gs]) * _silu(z[:, gs])
            y = y * lax.rsqrt(jnp.mean(y * y, axis=-1, keepdims=True) + EPS)
            outs.append(y * norm_g[:, gs])
        y_ref[0, rows, :] = jnp.concatenate(outs, axis=-1).astype(y_ref.dtype)
        return carry

    lax.fori_loop(0, nsub, chunk, 0)


def _ssd_mix(proj, conv_w, conv_b, head_params, chan_params, expand, *, pad):
    bsz, lp, _ = proj.shape
    tr = _rows_per_step(lp)

    def col(off, width):
        return pl.BlockSpec((1, tr, width), lambda b, s, o=off // width: (b, s, o))

    def full(shape):
        return pl.BlockSpec(shape, lambda b, s: (0,) * len(shape))

    return pl.pallas_call(
        functools.partial(_ssd_kernel, pad=pad),
        grid=(bsz, lp // tr),
        in_specs=[col(COL_Z, B_WIDTH), col(COL_XBC, B_CONV_DIM), col(COL_DT, LANES),
                  full((8, B_CONV_DIM)), full((1, B_CONV_DIM)), full((8, LANES)), full((8, B_WIDTH)),
                  full((LANES, B_WIDTH))],
        out_specs=pl.BlockSpec((1, tr, B_WIDTH), lambda b, s: (b, s, 0)),
        out_shape=jax.ShapeDtypeStruct((bsz, lp, B_WIDTH), BF16),
        scratch_shapes=[pltpu.VMEM((SUBLANES, B_CONV_DIM), F32), pltpu.VMEM((B_STATE, B_WIDTH), F32)],
        compiler_params=pltpu.CompilerParams(dimension_semantics=("parallel", "arbitrary"),
                                             vmem_limit_bytes=VMEM_LIMIT),
        name="ssd",
    )(proj, proj, proj, conv_w, conv_b, head_params, chan_params, expand)


def _retention_kernel(q_ref, k_ref, v_ref, g_ref, cos_ref, sin_ref, dec_ref, y_ref, st_ref):
    ck = CHUNK
    nch = q_ref.shape[1] // ck

    @pl.when(pl.program_id(1) == 0)
    def _():
        st_ref[...] = jnp.zeros(st_ref.shape, F32)

    even = (lax.broadcasted_iota(jnp.int32, (1, C_QK), 1) % 2) == 0
    cos_t = cos_ref[...]
    sin_t = sin_ref[...]

    def rot(t):
        partner = jnp.where(even, pltpu.roll(t, C_QK - 1, 1), pltpu.roll(t, 1, 1))
        return t * cos_t + partner * sin_t

    q = rot(q_ref[0].astype(F32))
    k = rot(k_ref[0].astype(F32)) * (C_QK_HEAD ** -0.5)
    v = v_ref[0]
    gate = _silu(g_ref[0].astype(F32))

    pairs = [(c, h) for c in range(nch) for h in range(C_HEADS)]
    rows = [slice(c * ck, (c + 1) * ck) for c in range(nch)]
    qh = {(c, h): q[rows[c], h * C_QK_HEAD:(h + 1) * C_QK_HEAD] for c, h in pairs}
    kh = {(c, h): k[rows[c], h * C_QK_HEAD:(h + 1) * C_QK_HEAD] for c, h in pairs}
    vh = {(c, h): v[rows[c], h * C_V_HEAD:(h + 1) * C_V_HEAD].astype(BF16) for c, h in pairs}
    scores = {p: _mm(qh[p], kh[p], NT) * dec_ref[3 * p[1]] for p in pairs}
    y_intra = {p: _dg(scores[p].astype(BF16), vh[p], NN) for p in pairs}
    kv = {p: _dg((kh[p] * dec_ref[3 * p[1] + 2]).astype(BF16), vh[p], TN) for p in pairs}
    s_in = {}
    for h in range(C_HEADS):
        chunk_decay = (1.0 - 2.0 ** (-5.0 - h)) ** ck
        st = st_ref[h]
        for c in range(nch):
            s_in[(c, h)] = st
            st = st * chunk_decay + kv[(c, h)]
        st_ref[h] = st
    y = {p: y_intra[p] + _mm(qh[p] * dec_ref[3 * p[1] + 1], s_in[p]) for p in pairs}
    yc = {p: y[p] - jnp.mean(y[p], axis=-1, keepdims=True) for p in pairs}
    yn = {p: yc[p] * lax.rsqrt(jnp.mean(yc[p] * yc[p], axis=-1, keepdims=True) + EPS) for p in pairs}
    out = jnp.concatenate([jnp.concatenate([yn[(c, h)] for h in range(C_HEADS)], axis=-1)
                           for c in range(nch)], axis=0)
    y_ref[0] = (gate * out).astype(y_ref.dtype)


def _retention_decay_tables():
    idx = np.arange(CHUNK, dtype=np.float64)
    rel = idx[:, None] - idx[None, :]
    tabs = []
    for h in range(C_HEADS):
        log_gamma = math.log(1.0 - 2.0 ** (-5.0 - h))
        tabs.append(np.where(rel >= 0, np.exp(np.maximum(rel, 0.0) * log_gamma), 0.0))
        tabs.append(np.broadcast_to(np.exp((idx + 1.0) * log_gamma)[:, None], (CHUNK, C_QK_HEAD)))
        tabs.append(np.broadcast_to(np.exp((CHUNK - 1.0 - idx) * log_gamma)[:, None], (CHUNK, C_QK_HEAD)))
    return jnp.asarray(np.stack(tabs), F32)


def _retention_mix(proj, cos_t, sin_t):
    bsz, lp, _ = proj.shape
    tr = _rows_per_step(lp)

    def col(off, width):
        return pl.BlockSpec((1, tr, width), lambda b, s, o=off // width: (b, s, o))

    tab = pl.BlockSpec((tr, C_QK), lambda b, s: (s, 0))
    return pl.pallas_call(
        _retention_kernel,
        grid=(bsz, lp // tr),
        in_specs=[col(COL_RET_Q, C_QK), col(COL_RET_K, C_QK), col(COL_RET_V, C_WIDTH),
                  col(COL_RET_G, C_WIDTH), tab, tab,
                  pl.BlockSpec((3 * C_HEADS, CHUNK, CHUNK), lambda b, s: (0, 0, 0))],
        out_specs=pl.BlockSpec((1, tr, C_WIDTH), lambda b, s: (b, s, 0)),
        out_shape=jax.ShapeDtypeStruct((bsz, lp, C_WIDTH), BF16),
        scratch_shapes=[pltpu.VMEM((C_HEADS, C_QK_HEAD, C_V_HEAD), F32)],
        compiler_params=pltpu.CompilerParams(dimension_semantics=("parallel", "arbitrary"),
                                             vmem_limit_bytes=VMEM_LIMIT),
        name="retention",
    )(proj, proj, proj, proj, cos_t, sin_t, _retention_decay_tables())


def _merge_kernel(ya_ref, yb_ref, yc_ref, g0_ref, g1_ref, g2_ref, gb_ref, h_ref,
                  wa_ref, wb0_ref, wb1_ref, wc_ref, wo_ref, o_ref):
    gb = gb_ref[...]
    d = D_MODEL
    yb = yb_ref[...]
    branch_b = _dg(yb[:, :d], wb0_ref[...], NN) + _dg(yb[:, d:], wb1_ref[...], NN)
    merged = (_sigmoid(g0_ref[...].astype(F32) + gb[:, :d]) * _dg(ya_ref[...], wa_ref[...], NN)
              + _sigmoid(g1_ref[...].astype(F32) + gb[:, d:2 * d]) * branch_b
              + _sigmoid(g2_ref[...].astype(F32) + gb[:, 2 * d:]) * _dg(yc_ref[...], wc_ref[...], NN))
    o_ref[...] = h_ref[...] + _dg(merged.astype(BF16), wo_ref[...], NN)


def _merge(ya, yb, yc, proj2d, gate_bias, h2d, w_branch, wo):
    m, d = h2d.shape
    assert A_WIDTH == d and B_WIDTH == 2 * d and C_WIDTH == d
    tm = _largest_divisor(m, 352)

    def rows(width):
        return pl.BlockSpec((tm, width), lambda i: (i, 0))

    def gcol(j):
        return pl.BlockSpec((tm, d), lambda i, o=COL_GATE // d + j: (i, o))

    def full(shape):
        return pl.BlockSpec(shape, lambda i: (0, 0))

    def wblock(k):
        return pl.BlockSpec((d, d), lambda i, k=k: (k, 0))

    return pl.pallas_call(
        _merge_kernel,
        grid=(m // tm,),
        in_specs=[rows(A_WIDTH), rows(B_WIDTH), rows(C_WIDTH), gcol(0), gcol(1), gcol(2),
                  full((1, GATE_IN)), rows(d),
                  wblock(0), wblock(1), wblock(2), wblock(3), full((d, d))],
        out_specs=rows(d),
        out_shape=jax.ShapeDtypeStruct((m, d), F32),
        compiler_params=pltpu.CompilerParams(dimension_semantics=("parallel",),
                                             vmem_limit_bytes=VMEM_LIMIT),
        name="merge",
    )(ya, yb, yc, proj2d, proj2d, proj2d, gate_bias, h2d, w_branch, w_branch, w_branch, w_branch, wo)


def _matmul_residual_kernel(a_ref, w_ref, h_ref, o_ref):
    o_ref[...] = h_ref[...] + _dg(a_ref[...], w_ref[...], NN)


def _matmul_residual(a, w, h2d):
    m, d = h2d.shape
    kdim = a.shape[1]
    tm = _largest_divisor(m, 1056)
    return pl.pallas_call(
        _matmul_residual_kernel,
        grid=(m // tm,),
        in_specs=[pl.BlockSpec((tm, kdim), lambda i: (i, 0)), pl.BlockSpec((kdim, d), lambda i: (0, 0)),
                  pl.BlockSpec((tm, d), lambda i: (i, 0))],
        out_specs=pl.BlockSpec((tm, d), lambda i: (i, 0)),
        out_shape=jax.ShapeDtypeStruct((m, d), F32),
        compiler_params=pltpu.CompilerParams(dimension_semantics=("parallel",),
                                             vmem_limit_bytes=VMEM_LIMIT),
        name="ffn_down",
    )(a, w, h2d)


def _final_norm_kernel(h_ref, g_ref, o_ref, *, skip):
    x = h_ref[0, skip:, :]
    o_ref[0] = x * lax.rsqrt(jnp.mean(x * x, axis=-1, keepdims=True) + EPS) * g_ref[...]


def _final_norm(h3d, g, *, skip, seq):
    bsz, lp, d = h3d.shape
    assert lp == skip + seq
    return pl.pallas_call(
        functools.partial(_final_norm_kernel, skip=skip),
        grid=(bsz,),
        in_specs=[pl.BlockSpec((1, lp, d), lambda b: (b, 0, 0)), pl.BlockSpec((1, d), lambda b: (0, 0))],
        out_specs=pl.BlockSpec((1, seq, d), lambda b: (b, 0, 0)),
        out_shape=jax.ShapeDtypeStruct((bsz, seq, d), F32),
        compiler_params=pltpu.CompilerParams(dimension_semantics=("parallel",),
                                             vmem_limit_bytes=VMEM_LIMIT),
        name="final_norm",
    )(h3d, g.reshape(1, d))


def _pad_cols(t, width):
    return jnp.pad(t, [(0, 0)] * (t.ndim - 1) + [(0, width - t.shape[-1])])


def _pad_rows(t, rows):
    return jnp.pad(t, [(0, rows - t.shape[0])] + [(0, 0)] * (t.ndim - 1))


def _pack_in_proj(w, w_vres):
    a, b, c, g = (w[:, :A_IN], w[:, A_IN:A_IN + B_IN], w[:, A_IN + B_IN:A_IN + B_IN + C_IN],
                  w[:, A_IN + B_IN + C_IN:])
    aw = 3 * A_WIDTH
    parts = [b[:, B_WIDTH:B_WIDTH + B_CONV_DIM], g, b[:, :B_WIDTH],
             c[:, 2 * C_QK:2 * C_QK + C_WIDTH], c[:, 2 * C_QK + C_WIDTH:], c[:, :C_QK], c[:, C_QK:2 * C_QK],
             a[:, :aw], a[:, aw:aw + LANES], a[:, aw + LANES:],
             _pad_cols(w_vres, LANES), _pad_cols(b[:, B_WIDTH + B_CONV_DIM:], LANES)]
    out = jnp.concatenate(parts, axis=1)
    assert out.shape[1] == N_PROJ
    return out.astype(BF16)


def kernel(x, meta, norm_mix, norm_ffn, norm_final, w_in, w_in_vres, gate_bias, rwkv_mu, rwkv_mu_vres, rwkv_w0, rwkv_w2, rwkv_a0, rwkv_a2, rwkv_v0, rwkv_v2, rwkv_g2, rwkv_k_k, rwkv_k_a, rwkv_r_k, rwkv_ln_g, rwkv_ln_b, ssm_conv_w, ssm_conv_b, ssm_dt_bias, ssm_a_log, ssm_d, ssm_norm_g, w_branch, w_out, ffn_w_up, ffn_conv_w, ffn_conv_b, ffn_w_down):
    bsz, seq, d = x.shape
    depth = w_in.shape[0]
    length = N_META + seq
    pad = (-length) % CHUNK
    lp = pad + length
    assert (pad + N_META) % CHUNK == 0 and seq % CHUNK == 0
    m = bsz * lp

    h = jnp.concatenate([jnp.zeros((bsz, pad, d), F32),
                         jnp.broadcast_to(meta.astype(F32), (bsz, N_META, d)), x.astype(F32)], axis=1)
    h = h.reshape(m, d)

    pos = jnp.arange(length, dtype=F32)
    inv_freq = 1.0 / (ROPE_BASE ** jnp.linspace(0.0, 1.0, C_QK_HEAD // 2, dtype=F32))
    ang = pos[:, None] * inv_freq[None, :]
    cos_t = jnp.tile(jnp.repeat(jnp.cos(ang), 2, axis=1), (1, C_HEADS))
    sin_t = jnp.tile((jnp.repeat(jnp.sin(ang), 2, axis=1)
                      * jnp.tile(jnp.array([-1.0, 1.0], F32), C_QK_HEAD // 2)), (1, C_HEADS))
    cos_t = jnp.pad(cos_t, ((pad, 0), (0, 0)))
    sin_t = jnp.pad(sin_t, ((pad, 0), (0, 0)))

    expand = (jnp.arange(LANES)[:, None] == (jnp.arange(B_WIDTH)[None, :] // B_HEAD)).astype(BF16)

    heads = RWKV_HEADS_PER_STEP
    gw = heads * A_HEAD
    ngrp = A_WIDTH // gw
    aw = 3 * A_WIDTH
    v_first = None
    for l in range(depth):
        has_vres = l > 0
        w_vres = w_in_vres[l - 1] if has_vres else jnp.zeros((d, A_VRES_LORA), F32)
        proj = _norm_matmul(h, norm_mix[l], _pack_in_proj(w_in[l], w_vres), lp=lp, pad=pad, tn=512,
                            out_dtype=BF16)
        proj3 = proj.reshape(bsz, lp, N_PROJ)

        mu = rwkv_mu[l]
        zeros_w = jnp.zeros((A_WIDTH,), F32)
        pvec = jnp.stack([mu[:A_WIDTH], mu[A_WIDTH:2 * A_WIDTH], mu[2 * A_WIDTH:aw],
                          rwkv_w0[l], rwkv_a0[l], rwkv_k_k[l], rwkv_k_a[l], rwkv_r_k[l].reshape(-1),
                          rwkv_ln_g[l], rwkv_ln_b[l], rwkv_v0[l - 1] if has_vres else zeros_w]
                         + [zeros_w] * 5)
        mu_vr = _pad_cols(rwkv_mu_vres[l - 1], LANES) if has_vres else jnp.zeros((LANES,), F32)
        lmu = _pad_rows(jnp.stack([mu[aw:aw + LANES], mu[aw + LANES:], mu_vr]), 8)
        w2g = rwkv_w2[l].reshape(A_DECAY_LORA, ngrp, gw).transpose(1, 0, 2)
        a2g = rwkv_a2[l].reshape(A_ICLR_LORA, ngrp, gw).transpose(1, 0, 2)
        wwa = jnp.concatenate([jnp.concatenate([w2g, jnp.zeros_like(w2g)], axis=2),
                               jnp.concatenate([jnp.zeros_like(a2g), a2g], axis=2)], axis=1).astype(BF16)
        v2 = _pad_rows(rwkv_v2[l - 1], LANES).astype(BF16) if has_vres else None
        y_a, v_first = _rwkv_mix(proj3, v_first, pvec, lmu, wwa, rwkv_g2[l].astype(BF16), v2,
                                 has_vres=has_vres)

        head_params = _pad_rows(jnp.stack([_pad_cols(ssm_dt_bias[l], LANES), _pad_cols(ssm_a_log[l], LANES)]), 8)
        chan_params = _pad_rows(jnp.stack([jnp.repeat(ssm_d[l], B_HEAD), ssm_norm_g[l]]), 8)
        y_b = _ssd_mix(proj3, _pad_rows(ssm_conv_w[l], 8), ssm_conv_b[l].reshape(1, -1), head_params,
                       chan_params, expand, pad=pad)

        y_c = _retention_mix(proj3, cos_t, sin_t)

        h = _merge(y_a.reshape(m, A_WIDTH), y_b.reshape(m, B_WIDTH), y_c.reshape(m, C_WIDTH), proj,
                   gate_bias[l].reshape(1, -1), h, w_branch[l].astype(BF16), w_out[l].astype(BF16))

        act = _ffn_up(h, norm_ffn[l], ffn_w_up[l].astype(BF16), _pad_rows(ffn_conv_w[l], 8),
                      ffn_conv_b[l].reshape(1, -1), lp=lp, pad=pad)
        h = _matmul_residual(act, ffn_w_down[l].astype(BF16), h)

    return _final_norm(h.reshape(bsz, lp, d), norm_final, skip=pad + N_META, seq=seq)
```
